```python
import math
import jax, jax.numpy as jnp
from jax import lax
import numpy as np

D_MODEL = 2048
BATCH = 8
SEQ = 4096
DEPTH = 4

N_MIXERS = 2
A_HEADS = 32
A_KV_HEADS = 4
A_HEAD_DIM = 64
WINDOW = 128
BLOCK = 128
A_QKV = (A_HEADS + 2 * A_KV_HEADS) * A_HEAD_DIM
REL_BUCKETS = 32
REL_MAX_DIST = 128
B_HEADS = 16
Q_LORA = 512
KV_LORA = 512
QK_NOPE = 128
QK_ROPE = 64
V_HEAD = 128
ROPE_THETA = 10000.0
B_DOWN = Q_LORA + KV_LORA + QK_ROPE
D_FF = 5632
N_EXPERTS = 8
TOP_K = 2
D_FF_EXPERT = 5632
N_MOD = 6
EPS = 1e-6
NEG_INF = -1e30

kernel_name = "hybrid_swa_sink_mla_moe_adaln"


def _rmsnorm(x, g):
    xf = x.astype(jnp.float32)
    y = xf * lax.rsqrt(jnp.mean(xf * xf, axis=-1, keepdims=True) + EPS)
    return (y * g.astype(jnp.float32)).astype(x.dtype)


def _modulate(h, shift, scale):
    return h * (1 + scale[:, None, :]) + shift[:, None, :]


def _t5_bucket(n):
    max_exact = REL_BUCKETS // 2
    nf = jnp.maximum(n, 1).astype(jnp.float32)
    large = max_exact + (jnp.log(nf / max_exact) / math.log(REL_MAX_DIST / max_exact)
                         * (REL_BUCKETS - max_exact)).astype(jnp.int32)
    large = jnp.minimum(large, REL_BUCKETS - 1)
    return jnp.where(n < max_exact, n, large)


def _rope(t, cos, sin):
    t1, t2 = jnp.split(t, 2, axis=-1)
    return jnp.concatenate([t1 * cos - t2 * sin, t1 * sin + t2 * cos], axis=-1).astype(t.dtype)


def _swa_mixer(h, wqkv, q_g, k_g, sinks, wo, rel_bias):
    B, S, _ = h.shape
    nb = S // BLOCK
    G = A_HEADS // A_KV_HEADS
    qkv = h @ wqkv
    q, k, v = jnp.split(qkv, [A_HEADS * A_HEAD_DIM, (A_HEADS + A_KV_HEADS) * A_HEAD_DIM], axis=-1)
    q = _rmsnorm(q.reshape(B, S, A_KV_HEADS, G, A_HEAD_DIM), q_g)
    k = _rmsnorm(k.reshape(B, S, A_KV_HEADS, A_HEAD_DIM), k_g)
    v = v.reshape(B, S, A_KV_HEADS, A_HEAD_DIM)
    qb = q.reshape(B, nb, BLOCK, A_KV_HEADS, G, A_HEAD_DIM).transpose(1, 0, 2, 3, 4, 5)

    def band(t):
        tb = t.reshape(B, nb, BLOCK, A_KV_HEADS, A_HEAD_DIM)
        prev = jnp.pad(tb, ((0, 0), (1, 0), (0, 0), (0, 0), (0, 0)))[:, :nb]
        return jnp.concatenate([prev, tb], axis=2).transpose(1, 0, 2, 3, 4)

    kb, vb = band(k), band(v)
    t_idx = jnp.arange(BLOCK)[:, None]
    j_idx = jnp.arange(2 * BLOCK)[None, :]
    dist = t_idx + BLOCK - j_idx
    in_window = (dist >= 0) & (dist < WINDOW)
    bias = rel_bias[_t5_bucket(jnp.maximum(dist, 0))]
    bias = bias.astype(jnp.float32).transpose(2, 0, 1).reshape(A_KV_HEADS, G, BLOCK, 2 * BLOCK)
    sink = sinks.astype(jnp.float32).reshape(A_KV_HEADS, G)[..., None, None]
    scale = A_HEAD_DIM ** -0.5

    def block_attn(args):
        blk, qi, ki, vi = args
        s = jnp.einsum('btkgd,bjkd->bkgtj', qi, ki).astype(jnp.float32) * scale + bias
        valid = in_window & (j_idx + blk * BLOCK >= BLOCK)
        s = jnp.where(valid, s, NEG_INF)
        m = jnp.maximum(jnp.max(s, axis=-1, keepdims=True), sink)
        p = jnp.exp(s - m)
        denom = jnp.sum(p, axis=-1, keepdims=True) + jnp.exp(sink - m)
        p = (p / denom).astype(vi.dtype)
        return jnp.einsum('bkgtj,bjkd->btkgd', p, vi)

    o = lax.map(block_attn, (jnp.arange(nb), qb, kb, vb))
    o = o.transpose(1, 0, 2, 3, 4, 5).reshape(B, S, A_HEADS * A_HEAD_DIM)
    return o @ wo


def _mla_mixer(h, wdown, q_a_g, kv_a_g, wuq, wukv, q_g, k_g, wo):
    B, S, _ = h.shape
    nb = S // BLOCK
    lat = h @ wdown
    cq, ckv, k_pe = jnp.split(lat, [Q_LORA, Q_LORA + KV_LORA], axis=-1)
    q = (_rmsnorm(cq, q_a_g) @ wuq).reshape(B, S, B_HEADS, QK_NOPE + QK_ROPE)
    kv = (_rmsnorm(ckv, kv_a_g) @ wukv).reshape(B, S, B_HEADS, QK_NOPE + V_HEAD)
    q_nope, q_pe = jnp.split(q, [QK_NOPE], axis=-1)
    k_nope, v = jnp.split(kv, [QK_NOPE], axis=-1)
    q_nope = _rmsnorm(q_nope, q_g[:QK_NOPE])
    q_pe = _rmsnorm(q_pe, q_g[QK_NOPE:])
    k_nope = _rmsnorm(k_nope, k_g[:QK_NOPE])
    k_pe = _rmsnorm(k_pe, k_g[QK_NOPE:])
    pos = jnp.arange(S, dtype=jnp.float32)
    inv = ROPE_THETA ** (-jnp.arange(0, QK_ROPE, 2, dtype=jnp.float32) / QK_ROPE)
    ang = pos[:, None] * inv[None, :]
    cos, sin = jnp.cos(ang), jnp.sin(ang)
    q_pe = _rope(q_pe, cos[:, None, :], sin[:, None, :])
    k_pe = _rope(k_pe, cos, sin)
    qn_b = q_nope.reshape(B, nb, BLOCK, B_HEADS, QK_NOPE).transpose(1, 0, 2, 3, 4)
    qp_b = q_pe.reshape(B, nb, BLOCK, B_HEADS, QK_ROPE).transpose(1, 0, 2, 3, 4)
    kpos = jnp.arange(S)
    scale = (QK_NOPE + QK_ROPE) ** -0.5

    def block_attn(args):
        blk, qn, qp = args
        qpos = blk * BLOCK + jnp.arange(BLOCK)
        s = (jnp.einsum('bthd,bshd->bhts', qn, k_nope).astype(jnp.float32)
             + jnp.einsum('bthr,bsr->bhts', qp, k_pe).astype(jnp.float32)) * scale
        s = jnp.where(kpos[None, :] <= qpos[:, None], s, NEG_INF)
        p = jax.nn.softmax(s, axis=-1).astype(v.dtype)
        return jnp.einsum('bhts,bshd->bthd', p, v)

    o = lax.map(block_attn, (jnp.arange(nb), qn_b, qp_b))
    o = o.transpose(1, 0, 2, 3, 4).reshape(B, S, B_HEADS * V_HEAD)
    return o @ wo


def _swiglu(h, wg, wu, wd):
    return (jax.nn.silu(h @ wg) * (h @ wu)) @ wd


def _moe(h, w_router, b_router, wg, wu, wd):
    B, S, D = h.shape
    t = h.reshape(B * S, D)
    logits = (t @ w_router).astype(jnp.float32) + b_router.astype(jnp.float32)
    top_val, top_idx = lax.top_k(logits, TOP_K)
    top_w = jax.nn.softmax(top_val, axis=-1)
    gates = jnp.sum(jax.nn.one_hot(top_idx, N_EXPERTS, dtype=jnp.float32) * top_w[..., None], axis=1)
    gates = gates.astype(h.dtype)
    out = jnp.zeros_like(t)
    for e in range(N_EXPERTS):
        out = out + gates[:, e:e + 1] * _swiglu(t, wg[e], wu[e], wd[e])
    return out.reshape(B, S, D)


def setup_inputs(seed: int = 0) -> dict:
    key = jax.random.key(seed)
    ks = iter(jax.random.split(key, 40))
    nA = (DEPTH + 1) // 2
    nB = DEPTH // 2

    def nrm(shape, fan_in, mult=1.0):
        return jax.random.normal(next(ks), shape, jnp.float32) * (mult * fan_in ** -0.5)

    def gain(shape):
        return 1.0 + 0.05 * jax.random.normal(next(ks), shape, jnp.float32)

    return {
        "x": jax.random.normal(next(ks), (BATCH, SEQ, D_MODEL), jnp.float32),
        "c": jax.random.normal(next(ks), (BATCH, D_MODEL), jnp.float32),
        "rel_bias": 0.5 * jax.random.normal(next(ks), (REL_BUCKETS, A_HEADS), jnp.float32),
        "ada_w": nrm((DEPTH, D_MODEL, N_MOD * D_MODEL), D_MODEL, 0.5),
        "ada_b": 0.02 * jax.random.normal(next(ks), (DEPTH, N_MOD * D_MODEL), jnp.float32),
        "norm_g": gain((DEPTH, 2, D_MODEL)),
        "a_wqkv": nrm((nA, D_MODEL, A_QKV), D_MODEL),
        "a_q_norm": gain((nA, A_HEAD_DIM)),
        "a_k_norm": gain((nA, A_HEAD_DIM)),
        "a_sinks": 0.5 * jax.random.normal(next(ks), (nA, A_HEADS), jnp.float32),
        "a_wo": nrm((nA, A_HEADS * A_HEAD_DIM, D_MODEL), A_HEADS * A_HEAD_DIM),
        "b_wdown": nrm((nB, D_MODEL, B_DOWN), D_MODEL),
        "b_q_a_norm": gain((nB, Q_LORA)),
        "b_kv_a_norm": gain((nB, KV_LORA)),
        "b_wuq": nrm((nB, Q_LORA, B_HEADS * (QK_NOPE + QK_ROPE)), Q_LORA),
        "b_wukv": nrm((nB, KV_LORA, B_HEADS * (QK_NOPE + V_HEAD)), KV_LORA),
        "b_q_norm": gain((nB, QK_NOPE + QK_ROPE)),
        "b_k_norm": gain((nB, QK_NOPE + QK_ROPE)),
        "b_wo": nrm((nB, B_HEADS * V_HEAD, D_MODEL), B_HEADS * V_HEAD),
        "ffn_wg": nrm((nA, D_MODEL, D_FF), D_MODEL),
        "ffn_wu": nrm((nA, D_MODEL, D_FF), D_MODEL),
        "ffn_wd": nrm((nA, D_FF, D_MODEL), D_FF),
        "moe_router": nrm((nB, D_MODEL, N_EXPERTS), D_MODEL),
        "moe_router_b": 0.01 * jax.random.normal(next(ks), (nB, N_EXPERTS), jnp.float32),
        "moe_wg": nrm((nB, N_EXPERTS, D_MODEL, D_FF_EXPERT), D_MODEL),
        "moe_wu": nrm((nB, N_EXPERTS, D_MODEL, D_FF_EXPERT), D_MODEL),
        "moe_wd": nrm((nB, N_EXPERTS, D_FF_EXPERT, D_MODEL), D_FF_EXPERT),
    }


def reference(x, c, rel_bias, ada_w, ada_b, norm_g,
              a_wqkv, a_q_norm, a_k_norm, a_sinks, a_wo,
              b_wdown, b_q_a_norm, b_kv_a_norm, b_wuq, b_wukv, b_q_norm, b_k_norm, b_wo,
              ffn_wg, ffn_wu, ffn_wd,
              moe_router, moe_router_b, moe_wg, moe_wu, moe_wd):
    c_act = jax.nn.silu(c)
    for layer in range(DEPTH):
        i = layer // N_MIXERS
        mod = c_act @ ada_w[layer] + ada_b[layer]
        shift1, scale1, gate1, shift2, scale2, gate2 = jnp.split(mod, N_MOD, axis=-1)
        h = _modulate(_rmsnorm(x, norm_g[layer, 0]), shift1, scale1)
        if layer % N_MIXERS == 0:
            y = _swa_mixer(h, a_wqkv[i], a_q_norm[i], a_k_norm[i], a_sinks[i], a_wo[i], rel_bias)
        else:
            y = _mla_mixer(h, b_wdown[i], b_q_a_norm[i], b_kv_a_norm[i], b_wuq[i], b_wukv[i],
                           b_q_norm[i], b_k_norm[i], b_wo[i])
        x = x + gate1[:, None, :] * y
        h = _modulate(_rmsnorm(x, norm_g[layer, 1]), shift2, scale2)
        if layer % 2 == 0:
            y = _swiglu(h, ffn_wg[i], ffn_wu[i], ffn_wd[i])
        else:
            y = _moe(h, moe_router[i], moe_router_b[i], moe_wg[i], moe_wu[i], moe_wd[i])
        x = x + gate2[:, None, :] * y
    return x
```

```python
import functools
import math

import jax
import jax.numpy as jnp
from jax import lax
from jax.experimental import pallas as pl
from jax.experimental.pallas import tpu as pltpu

EPS = 1e-6
NEG_INF = -1e30
WINDOW = 128
BLOCK = 128
REL_MAX_DIST = 128
ROPE_THETA = 10000.0
TOP_K = 2

F32 = jnp.float32
BF16 = jnp.bfloat16
MIB = 1024 * 1024

_NT = (((1,), (1,)), ((), ()))


def _params(sem, vmem_mib):
    return pltpu.CompilerParams(dimension_semantics=sem, vmem_limit_bytes=vmem_mib * MIB)


def _pick(n, pref):
    t = min(pref, n)
    while n % t:
        t //= 2
    return t


def _rms(x):
    return x * lax.rsqrt(jnp.mean(x * x, axis=-1, keepdims=True) + EPS)


def _norm_mod(x, g, scale, shift):
    return (_rms(x) * g) * (1.0 + scale) + shift


def _silu(x):
    return x * (1.0 / (1.0 + jnp.exp(-x)))


def _resident(shape, index_map):
    return pl.BlockSpec(shape, index_map, pipeline_mode=pl.Buffered(1))


def _adaln_kernel(c_ref, w_ref, b_ref, o_ref):
    ca = _silu(c_ref[...]).astype(BF16)
    o_ref[...] = jnp.dot(ca, w_ref[...].astype(BF16), preferred_element_type=F32) + b_ref[...]


def _adaln(c_pad, ada_w, ada_b):
    L, D, N6 = ada_w.shape
    R = c_pad.shape[0]
    tn = _pick(N6, 1536)
    return pl.pallas_call(
        _adaln_kernel,
        grid=(L, N6 // tn),
        in_specs=[
            pl.BlockSpec((R, D), lambda l, j: (0, 0)),
            pl.BlockSpec((None, D, tn), lambda l, j: (l, 0, j)),
            pl.BlockSpec((None, 1, tn), lambda l, j: (l, 0, j)),
        ],
        out_specs=pl.BlockSpec((None, R, tn), lambda l, j: (l, 0, j)),
        out_shape=jax.ShapeDtypeStruct((L, R, N6), F32),
        compiler_params=_params(("arbitrary", "arbitrary"), 48),
        name="adaln",
    )(c_pad, ada_w, ada_b.reshape(L, 1, N6))


def _normmod_kernel(x_ref, g_ref, sc_ref, sh_ref, o_ref):
    o_ref[...] = _norm_mod(x_ref[...], g_ref[...], sc_ref[...], sh_ref[...]).astype(o_ref.dtype)


def _row_vec_spec(D, tm, S):
    return pl.BlockSpec((None, 1, D), lambda i: ((i * tm) // S, 0, 0))


def _normmod(x2, g, scale, shift, S, out_dtype):
    N, D = x2.shape
    tm = _pick(S, 512)
    return pl.pallas_call(
        _normmod_kernel,
        grid=(N // tm,),
        in_specs=[
            pl.BlockSpec((tm, D), lambda i: (i, 0)),
            pl.BlockSpec((1, D), lambda i: (0, 0)),
            _row_vec_spec(D, tm, S),
            _row_vec_spec(D, tm, S),
        ],
        out_specs=pl.BlockSpec((tm, D), lambda i: (i, 0)),
        out_shape=jax.ShapeDtypeStruct((N, D), out_dtype),
        compiler_params=_params(("arbitrary",), 40),
        name="normmod",
    )(x2, g, scale, shift)


def _matmul_kernel(a_ref, w_ref, o_ref, *, tn):
    a = a_ref[...]
    for j in range(0, o_ref.shape[-1], tn):
        o_ref[:, j:j + tn] = jnp.dot(a, w_ref[:, j:j + tn], preferred_element_type=F32).astype(o_ref.dtype)


def _matmul(a, w, out_dtype=BF16):
    N, K = a.shape
    Nout = w.shape[1]
    tm = _pick(N, 512)
    tn = _pick(Nout, 512)
    return pl.pallas_call(
        functools.partial(_matmul_kernel, tn=tn),
        grid=(N // tm,),
        in_specs=[pl.BlockSpec((tm, K), lambda i: (i, 0)), _resident((K, Nout), lambda i: (0, 0))],
        out_specs=pl.BlockSpec((tm, Nout), lambda i: (i, 0)),
        out_shape=jax.ShapeDtypeStruct((N, Nout), out_dtype),
        compiler_params=_params(("arbitrary",), 48),
        name="matmul",
    )(a, w)


def _t5_bucket(n, n_buckets):
    max_exact = n_buckets // 2
    nf = jnp.maximum(n, 1).astype(F32)
    large = max_exact + (jnp.log(nf / max_exact) / math.log(REL_MAX_DIST / max_exact)
                         * (n_buckets - max_exact)).astype(jnp.int32)
    large = jnp.minimum(large, n_buckets - 1)
    return jnp.where(n < max_exact, n, large)


def _swa_bias_tables(rel_bias, KV, G):
    t_idx = jnp.arange(BLOCK)[:, None]
    j_idx = jnp.arange(2 * BLOCK)[None, :]
    dist = t_idx + BLOCK - j_idx
    in_window = (dist >= 0) & (dist < WINDOW)
    bias = rel_bias[_t5_bucket(jnp.maximum(dist, 0), rel_bias.shape[0])].astype(F32)
    bias = bias.transpose(2, 0, 1)
    general = jnp.where(in_window[None], bias, NEG_INF)
    first = jnp.where((in_window & (j_idx >= BLOCK))[None], bias, NEG_INF)
    return jnp.stack([first, general]).reshape(2, KV, G * BLOCK, 2 * BLOCK)


def _swa_kernel(q_ref, kp_ref, kc_ref, vp_ref, vc_ref, bias_ref, qg_ref, kg_ref, sink_ref, o_ref, *, KV, G, Dh):
    scale = Dh ** -0.5
    k_all = jnp.concatenate([kp_ref[...], kc_ref[...]], axis=0).astype(F32)
    v_all = jnp.concatenate([vp_ref[...], vc_ref[...]], axis=0)
    q_all = q_ref[...].astype(F32)
    outs = []
    for kv in range(KV):
        kh = (_rms(k_all[:, kv * Dh:(kv + 1) * Dh]) * kg_ref[...]).astype(BF16)
        vh = v_all[:, kv * Dh:(kv + 1) * Dh]
        qs = jnp.concatenate([q_all[:, (kv * G + g) * Dh:(kv * G + g + 1) * Dh] for g in range(G)], axis=0)
        qs = ((_rms(qs) * qg_ref[...]) * scale).astype(BF16)
        s = lax.dot_general(qs, kh, _NT, preferred_element_type=F32) + bias_ref[kv]
        sink = sink_ref[kv]
        m = jnp.maximum(jnp.max(s, axis=-1, keepdims=True), sink)
        p = jnp.exp(s - m)
        denom = jnp.sum(p, axis=-1, keepdims=True) + jnp.exp(sink - m)
        o = jnp.dot(p.astype(BF16), vh, preferred_element_type=F32) * (1.0 / denom)
        outs += [o[g * BLOCK:(g + 1) * BLOCK] for g in range(G)]
    o_ref[...] = jnp.concatenate(outs, axis=1).astype(o_ref.dtype)


def _swa(qkv, bias_tab, q_g, k_g, sinks, B, S, H, KV, Dh):
    G = H // KV
    nb = S // BLOCK
    HD, KD = H * Dh, KV * Dh
    assert HD % KD == 0
    kcol = HD // KD
    qkv3 = qkv.reshape(B, S, HD + 2 * KD)
    sink_col = jnp.broadcast_to(sinks.astype(F32).reshape(KV, G, 1, 1), (KV, G, BLOCK, 1)).reshape(KV, G * BLOCK, 1)
    prev = lambda b, i: jnp.maximum(i - 1, 0)
    out = pl.pallas_call(
        functools.partial(_swa_kernel, KV=KV, G=G, Dh=Dh),
        grid=(B, nb),
        in_specs=[
            pl.BlockSpec((None, BLOCK, HD), lambda b, i: (b, i, 0)),
            pl.BlockSpec((None, BLOCK, KD), lambda b, i: (b, prev(b, i), kcol)),
            pl.BlockSpec((None, BLOCK, KD), lambda b, i: (b, i, kcol)),
            pl.BlockSpec((None, BLOCK, KD), lambda b, i: (b, prev(b, i), kcol + 1)),
            pl.BlockSpec((None, BLOCK, KD), lambda b, i: (b, i, kcol + 1)),
            pl.BlockSpec((None, KV, G * BLOCK, 2 * BLOCK), lambda b, i: (jnp.minimum(i, 1), 0, 0, 0)),
            pl.BlockSpec((1, Dh), lambda b, i: (0, 0)),
            pl.BlockSpec((1, Dh), lambda b, i: (0, 0)),
            pl.BlockSpec((KV, G * BLOCK, 1), lambda b, i: (0, 0, 0)),
        ],
        out_specs=pl.BlockSpec((None, BLOCK, HD), lambda b, i: (b, i, 0)),
        out_shape=jax.ShapeDtypeStruct((B, S, HD), BF16),
        compiler_params=_params(("arbitrary", "arbitrary"), 48),
        name="swa",
    )(qkv3, qkv3, qkv3, qkv3, qkv3, bias_tab, q_g.reshape(1, Dh), k_g.reshape(1, Dh), sink_col)
    return out.reshape(B * S, HD)


def _down_kernel(a_ref, w_ref, qag_ref, kvag_ref, kg_ref, kgs_ref, cos_ref, sin_ref,
                 cq_ref, ckv_ref, kpe_ref, *, QL, KVL, R):
    lat = jnp.dot(a_ref[...], w_ref[...], preferred_element_type=F32)
    cq_ref[...] = (_rms(lat[:, :QL]) * qag_ref[...]).astype(cq_ref.dtype)
    ckv_ref[...] = (_rms(lat[:, QL:QL + KVL]) * kvag_ref[...]).astype(ckv_ref.dtype)
    kpe = lat[:, QL + KVL:QL + KVL + R]
    kpe_sw = lat[:, QL + KVL + R:]
    r = lax.rsqrt(jnp.mean(kpe * kpe, axis=-1, keepdims=True) + EPS)
    kpe_ref[...] = (kpe * r * kg_ref[...]) * cos_ref[...] + (kpe_sw * r * kgs_ref[...]) * sin_ref[...]


def _mla_down(h, w_ext, qa_g, kva_g, kg_pe, kg_pe_sw, cos2, sin2, S, QL, KVL, R):
    N, D = h.shape
    tm = _pick(S, 512)
    W = w_ext.shape[1]
    ns = S // tm
    vec = lambda n: pl.BlockSpec((1, n), lambda i: (0, 0))
    pos_spec = pl.BlockSpec((tm, R), lambda i: (i % ns, 0))
    return pl.pallas_call(
        functools.partial(_down_kernel, QL=QL, KVL=KVL, R=R),
        grid=(N // tm,),
        in_specs=[pl.BlockSpec((tm, D), lambda i: (i, 0)), _resident((D, W), lambda i: (0, 0)),
                  vec(QL), vec(KVL), vec(R), vec(R), pos_spec, pos_spec],
        out_specs=[pl.BlockSpec((tm, QL), lambda i: (i, 0)), pl.BlockSpec((tm, KVL), lambda i: (i, 0)),
                   pl.BlockSpec((tm, R), lambda i: (i, 0))],
        out_shape=[jax.ShapeDtypeStruct((N, QL), BF16), jax.ShapeDtypeStruct((N, KVL), BF16),
                   jax.ShapeDtypeStruct((N, R), F32)],
        compiler_params=_params(("arbitrary",), 40),
        name="mla_down",
    )(h, w_ext, qa_g.reshape(1, QL), kva_g.reshape(1, KVL), kg_pe.reshape(1, R), kg_pe_sw.reshape(1, R), cos2, sin2)


def _up_kernel(cq_ref, ckv_ref, kpe_ref, cos_ref, sin_ref, wq_ref, wkv_ref, qgn_ref, qgp_ref, qgps_ref, kgn_ref,
               q_ref, k_ref, v_ref, *, NOPE, R, scale):
    h = pl.program_id(1)
    qh = jnp.dot(cq_ref[...], wq_ref[h], preferred_element_type=F32)
    qn, qp, qps = qh[:, :NOPE], qh[:, NOPE:NOPE + R], qh[:, NOPE + R:]
    q_ref[:, :NOPE] = ((_rms(qn) * qgn_ref[...]) * scale).astype(q_ref.dtype)
    rp = lax.rsqrt(jnp.mean(qp * qp, axis=-1, keepdims=True) + EPS)
    q_rot = (qp * rp * qgp_ref[...]) * cos_ref[...] + (qps * rp * qgps_ref[...]) * sin_ref[...]
    q_ref[:, NOPE:] = (q_rot * scale).astype(q_ref.dtype)
    kvh = jnp.dot(ckv_ref[...], wkv_ref[h], preferred_element_type=F32)
    k_ref[:, :NOPE] = (_rms(kvh[:, :NOPE]) * kgn_ref[...]).astype(k_ref.dtype)
    k_ref[:, NOPE:] = kpe_ref[...].astype(k_ref.dtype)
    v_ref[...] = kvh[:, NOPE:].astype(v_ref.dtype)


def _mla_up(cq, ckv, kpe, cos2, sin2, wq_h, wkv_h, qg_n, qg_p, qg_ps, kg_n, B, S, NOPE, R, V):
    N, QL = cq.shape
    KVL = ckv.shape[1]
    H = wq_h.shape[0]
    tm = _pick(S, 512)
    ns = S // tm
    scale = (NOPE + R) ** -0.5
    row = lambda n: pl.BlockSpec((tm, n), lambda i, h: (i, 0))
    pos_spec = pl.BlockSpec((tm, R), lambda i, h: (i % ns, 0))
    vec = lambda n: pl.BlockSpec((1, n), lambda i, h: (0, 0))
    head_out = lambda n: pl.BlockSpec((None, None, tm, n), lambda i, h: (i // ns, h, i % ns, 0))
    return pl.pallas_call(
        functools.partial(_up_kernel, NOPE=NOPE, R=R, scale=scale),
        grid=(N // tm, H),
        in_specs=[row(QL), row(KVL), row(R), pos_spec, pos_spec,
                  _resident(wq_h.shape, lambda i, h: (0, 0, 0)), _resident(wkv_h.shape, lambda i, h: (0, 0, 0)),
                  vec(NOPE), vec(R), vec(R), vec(NOPE)],
        out_specs=[head_out(NOPE + R), head_out(NOPE + R), head_out(V)],
        out_shape=[jax.ShapeDtypeStruct((B, H, S, NOPE + R), BF16), jax.ShapeDtypeStruct((B, H, S, NOPE + R), BF16),
                   jax.ShapeDtypeStruct((B, H, S, V), BF16)],
        compiler_params=_params(("arbitrary", "arbitrary"), 40),
        name="mla_up",
    )(cq, ckv, kpe, cos2, sin2, wq_h, wkv_h, qg_n.reshape(1, NOPE), qg_p.reshape(1, R), qg_ps.reshape(1, R),
      kg_n.reshape(1, NOPE))


def _flash_kernel(q_ref, k_ref, v_ref, o_ref, *, tq):
    qi = pl.program_id(2)
    q = q_ref[...]

    def step(ki, carry, diagonal):
        m, l, acc = carry
        start = pl.multiple_of(ki * tq, tq)
        k = k_ref[pl.ds(start, tq), :]
        v = v_ref[pl.ds(start, tq), :]
        s = lax.dot_general(q, k, _NT, preferred_element_type=F32)
        if diagonal:
            rows = lax.broadcasted_iota(jnp.int32, s.shape, 0)
            cols = lax.broadcasted_iota(jnp.int32, s.shape, 1)
            s = jnp.where(cols <= rows, s, NEG_INF)
        m_new = jnp.maximum(m, jnp.max(s, axis=-1, keepdims=True))
        alpha = jnp.exp(m - m_new)
        p = jnp.exp(s - m_new)
        l = alpha * l + jnp.sum(p, axis=-1, keepdims=True)
        acc = alpha * acc + jnp.dot(p.astype(BF16), v, preferred_element_type=F32)
        return m_new, l, acc

    carry = (jnp.full((tq, 1), NEG_INF, F32), jnp.zeros((tq, 1), F32), jnp.zeros((tq, v_ref.shape[-1]), F32))
    carry = lax.fori_loop(0, qi, lambda ki, c: step(ki, c, False), carry)
    _, l, acc = step(qi, carry, True)
    o_ref[...] = (acc * (1.0 / l)).astype(o_ref.dtype)


def _flash(q, k, v):
    B, H, S, DK = q.shape
    V = v.shape[-1]
    tq = _pick(S, 512)
    out = pl.pallas_call(
        functools.partial(_flash_kernel, tq=tq),
        grid=(B, H, S // tq),
        in_specs=[
            pl.BlockSpec((None, None, tq, DK), lambda b, h, i: (b, h, i, 0)),
            pl.BlockSpec((None, None, S, DK), lambda b, h, i: (b, h, 0, 0)),
            pl.BlockSpec((None, None, S, V), lambda b, h, i: (b, h, 0, 0)),
        ],
        out_specs=pl.BlockSpec((None, tq, V), lambda b, h, i: (b, i, h)),
        out_shape=jax.ShapeDtypeStruct((B, S, H * V), BF16),
        compiler_params=_params(("arbitrary", "arbitrary", "arbitrary"), 40),
        name="mla_flash",
    )(q, k, v)
    return out.reshape(B * S, H * V)


def _proj_res_kernel(a_ref, w_ref, x_ref, gate_ref, g_ref, sc_ref, sh_ref, xo_ref, h_ref):
    y = jnp.dot(a_ref[...], w_ref[...], preferred_element_type=F32)
    xn = x_ref[...] + gate_ref[...] * y
    xo_ref[...] = xn
    h_ref[...] = _norm_mod(xn, g_ref[...], sc_ref[...], sh_ref[...]).astype(h_ref.dtype)


def _proj_res(a, w, x2, gate, g, scale, shift, S, h_dtype):
    N, K = a.shape
    D = w.shape[1]
    tm = _pick(S, 256)
    return pl.pallas_call(
        _proj_res_kernel,
        grid=(N // tm,),
        in_specs=[
            pl.BlockSpec((tm, K), lambda i: (i, 0)),
            _resident((K, D), lambda i: (0, 0)),
            pl.BlockSpec((tm, D), lambda i: (i, 0)),
            _row_vec_spec(D, tm, S),
            pl.BlockSpec((1, D), lambda i: (0, 0)),
            _row_vec_spec(D, tm, S),
            _row_vec_spec(D, tm, S),
        ],
        out_specs=[pl.BlockSpec((tm, D), lambda i: (i, 0)), pl.BlockSpec((tm, D), lambda i: (i, 0))],
        out_shape=[jax.ShapeDtypeStruct((N, D), F32), jax.ShapeDtypeStruct((N, D), h_dtype)],
        compiler_params=_params(("arbitrary",), 48),
        name="proj_res",
    )(a, w, x2, gate, g, scale, shift)


def _ffn_kernel(te_ref, nu_ref, *refs, residual, nf):
    if residual:
        xs_ref, wg_ref, wu_ref, wd_ref, x_ref, gate_ref, g_ref, sc_ref, sh_ref, xo_ref, h_ref, acc_ref = refs
        xb_ref = xs_ref
    else:
        xs_ref, wg_ref, wu_ref, wd_ref, y_ref, acc_ref, xb_ref = refs
    m = pl.program_id(0)
    f = pl.program_id(1)
    active = m < nu_ref[0]

    @pl.when(f == 0)
    def _():
        acc_ref[...] = jnp.zeros_like(acc_ref)
        if not residual:
            xb_ref[...] = xs_ref[...].astype(BF16)

    @pl.when(active)
    def _():
        xb = xb_ref[...]
        gt = jnp.dot(xb, wg_ref[...], preferred_element_type=F32)
        up = jnp.dot(xb, wu_ref[...], preferred_element_type=F32)
        a = (_silu(gt) * up).astype(BF16)
        acc_ref[...] += jnp.dot(a, wd_ref[...], preferred_element_type=F32)

    @pl.when(f == nf - 1)
    def _():
        if residual:
            xn = x_ref[...] + gate_ref[...] * acc_ref[...]
            xo_ref[...] = xn
            h_ref[...] = _norm_mod(xn, g_ref[...], sc_ref[...], sh_ref[...]).astype(h_ref.dtype)
        else:
            y_ref[...] = acc_ref[...]


def _ffn(xs, wg, wu, wd, tile_expert, n_used, tm, epilogue=None):
    P, D = xs.shape
    E, _, F = wg.shape
    tf = _pick(F, 512)
    nf = F // tf
    nm = P // tm
    residual = epilogue is not None

    def fidx(m, f, te, nu):
        return jnp.where(m < nu[0], f, nf - 1)

    in_specs = [
        pl.BlockSpec((tm, D), lambda m, f, te, nu: (m, 0)),
        pl.BlockSpec((None, D, tf), lambda m, f, te, nu: (te[m], 0, fidx(m, f, te, nu))),
        pl.BlockSpec((None, D, tf), lambda m, f, te, nu: (te[m], 0, fidx(m, f, te, nu))),
        pl.BlockSpec((None, tf, D), lambda m, f, te, nu: (te[m], fidx(m, f, te, nu), 0)),
    ]
    args = [xs, wg, wu, wd]
    row = pl.BlockSpec((tm, D), lambda m, f, te, nu: (m, 0))
    if residual:
        x2, gate, g, scale, shift, S, h_dtype = epilogue
        bvec = pl.BlockSpec((None, 1, D), lambda m, f, te, nu: ((m * tm) // S, 0, 0))
        in_specs += [row, bvec, pl.BlockSpec((1, D), lambda m, f, te, nu: (0, 0)), bvec, bvec]
        args += [x2, gate, g, scale, shift]
        out_specs = [row, row]
        out_shape = [jax.ShapeDtypeStruct((P, D), F32), jax.ShapeDtypeStruct((P, D), h_dtype)]
    else:
        out_specs = row
        out_shape = jax.ShapeDtypeStruct((P, D), F32)
    return pl.pallas_call(
        functools.partial(_ffn_kernel, residual=residual, nf=nf),
        grid_spec=pltpu.PrefetchScalarGridSpec(
            num_scalar_prefetch=2,
            grid=(nm, nf),
            in_specs=in_specs,
            out_specs=out_specs,
            scratch_shapes=[pltpu.VMEM((tm, D), F32)] + ([] if residual else [pltpu.VMEM((tm, D), BF16)]),
        ),
        out_shape=out_shape,
        compiler_params=_params(("arbitrary", "arbitrary"), 56),
        name="ffn_res" if residual else "ffn_grouped",
    )(tile_expert, n_used, *args)


def _router_kernel(h_ref, w_ref, b_ref, mi_ref, mw_ref, cnt_ref, carry_ref, *, E):
    i = pl.program_id(0)
    tm = h_ref.shape[0]

    @pl.when(i == 0)
    def _():
        carry_ref[...] = jnp.zeros_like(carry_ref)

    hf = h_ref[...]
    h_hi = hf.astype(BF16)
    h_lo = (hf - h_hi.astype(F32)).astype(BF16)
    w = w_ref[...]
    l2 = lax.dot_general(w, h_hi, _NT, preferred_element_type=F32)
    l1 = lax.dot_general(w[:E], h_lo, _NT, preferred_element_type=F32)
    logits = (l2[:E] + l2[E:] + l1) + b_ref[...]
    eid = lax.broadcasted_iota(jnp.int32, (E, tm), 0)
    m1 = jnp.max(logits, axis=0, keepdims=True)
    i1 = jnp.min(jnp.where(logits == m1, eid, E), axis=0, keepdims=True)
    rest = jnp.where(eid == i1, -jnp.inf, logits)
    m2 = jnp.max(rest, axis=0, keepdims=True)
    i2 = jnp.min(jnp.where(rest == m2, eid, E), axis=0, keepdims=True)
    e2 = jnp.exp(m2 - m1)
    inv = 1.0 / (1.0 + e2)
    oh1 = eid == i1
    oh2 = eid == i2
    cnt = jnp.where(oh1 | oh2, 1.0, 0.0)
    r_i = lax.broadcasted_iota(jnp.int32, (tm, tm), 0)
    c_i = lax.broadcasted_iota(jnp.int32, (tm, tm), 1)
    tri = jnp.where(r_i <= c_i, 1.0, 0.0).astype(BF16)
    incl = jnp.dot(cnt.astype(BF16), tri, preferred_element_type=F32)
    rank = carry_ref[...] + incl - cnt
    r1 = jnp.sum(jnp.where(oh1, rank, 0.0), axis=0, keepdims=True)
    r2 = jnp.sum(jnp.where(oh2, rank, 0.0), axis=0, keepdims=True)
    carry = carry_ref[...] + jnp.sum(cnt, axis=1, keepdims=True)
    carry_ref[...] = carry
    zi = jnp.zeros((4, tm), jnp.int32)
    mi_ref[...] = jnp.concatenate([i1, i2, r1.astype(jnp.int32), r2.astype(jnp.int32), zi], axis=0)
    mw_ref[...] = jnp.concatenate([inv, e2 * inv, jnp.zeros((6, tm), F32)], axis=0)
    cnt_ref[...] = jnp.broadcast_to(carry.astype(jnp.int32), cnt_ref.shape)


def _router(h, w_hl, b):
    N, D = h.shape
    E = b.shape[0]
    tm = _pick(N, 512)
    return pl.pallas_call(
        functools.partial(_router_kernel, E=E),
        grid=(N // tm,),
        in_specs=[pl.BlockSpec((tm, D), lambda i: (i, 0)), pl.BlockSpec((2 * E, D), lambda i: (0, 0)),
                  pl.BlockSpec((E, 1), lambda i: (0, 0))],
        out_specs=[pl.BlockSpec((8, tm), lambda i: (0, i)), pl.BlockSpec((8, tm), lambda i: (0, i)),
                   pl.BlockSpec((E, 128), lambda i: (0, 0))],
        out_shape=[jax.ShapeDtypeStruct((8, N), jnp.int32), jax.ShapeDtypeStruct((8, N), F32),
                   jax.ShapeDtypeStruct((E, 128), jnp.int32)],
        scratch_shapes=[pltpu.VMEM((E, 1), F32)],
        compiler_params=_params(("arbitrary",), 32),
        name="router",
    )(h, w_hl, b.reshape(E, 1))


def _pos_kernel(mi_ref, cnt_ref, pos_ref, te_ref, *, E, T):
    shift = T.bit_length() - 1
    cnt = cnt_ref[:, 0:1]
    padded = ((cnt + (T - 1)) >> shift) << shift
    eid = lax.broadcasted_iota(jnp.int32, (E, 1), 0)
    off = jnp.zeros((E, 1), jnp.int32)
    for j in range(E):
        off = off + jnp.where(eid > j, padded[j:j + 1, :], 0)
    end = off + padded
    e1, e2 = mi_ref[0:1, :], mi_ref[1:2, :]
    o1 = jnp.zeros_like(e1)
    o2 = jnp.zeros_like(e2)
    for j in range(E):
        o1 = o1 + jnp.where(e1 == j, off[j:j + 1, :], 0)
        o2 = o2 + jnp.where(e2 == j, off[j:j + 1, :], 0)
    zeros = jnp.zeros((6, pos_ref.shape[1]), jnp.int32)
    pos_ref[...] = jnp.concatenate([o1 + mi_ref[2:3, :], o2 + mi_ref[3:4, :], zeros], axis=0)
    tstart = lax.broadcasted_iota(jnp.int32, (1, te_ref.shape[1]), 1) * T
    te = jnp.zeros_like(tstart)
    for j in range(E):
        te = te + jnp.where(end[j:j + 1, :] <= tstart, 1, 0)
    te = jnp.minimum(te, E - 1)
    n_used = jnp.broadcast_to(end[E - 1:E, :] >> shift, te.shape)
    te_ref[...] = jnp.concatenate([te, n_used, jnp.zeros((6, te.shape[1]), jnp.int32)], axis=0)


def _positions(meta_i, counts, T, n_tiles):
    N = meta_i.shape[1]
    E = counts.shape[0]
    ntp = ((n_tiles + 127) // 128) * 128
    return pl.pallas_call(
        functools.partial(_pos_kernel, E=E, T=T),
        out_shape=[jax.ShapeDtypeStruct((8, N), jnp.int32), jax.ShapeDtypeStruct((8, ntp), jnp.int32)],
        compiler_params=pltpu.CompilerParams(vmem_limit_bytes=32 * MIB),
        name="moe_positions",
    )(meta_i, counts)


def _dispatch_kernel(pos_ref, h_ref, xs_in_ref, xs_ref, sem):
    del xs_in_ref
    tm = h_ref.shape[0]

    def row_copy(r, k):
        return pltpu.make_async_copy(h_ref.at[pl.ds(r, 1), :], xs_ref.at[pl.ds(pos_ref[k, r], 1), :], sem)

    def start(r, c):
        for k in range(TOP_K):
            row_copy(r, k).start()
        return c

    def wait(r, c):
        for k in range(TOP_K):
            row_copy(r, k).wait()
        return c

    lax.fori_loop(0, tm, start, 0)
    lax.fori_loop(0, tm, wait, 0)


def _dispatch(pos, h, P):
    N, D = h.shape
    tm = _pick(N, 256)
    return pl.pallas_call(
        _dispatch_kernel,
        grid=(N // tm,),
        in_specs=[
            pl.BlockSpec((8, tm), lambda i: (0, i), memory_space=pltpu.SMEM),
            pl.BlockSpec((tm, D), lambda i: (i, 0)),
            pl.BlockSpec(memory_space=pl.ANY),
        ],
        out_specs=pl.BlockSpec(memory_space=pl.ANY),
        out_shape=jax.ShapeDtypeStruct((P, D), h.dtype),
        scratch_shapes=[pltpu.SemaphoreType.DMA(())],
        input_output_aliases={2: 0},
        compiler_params=_params(("arbitrary",), 32),
        name="moe_dispatch",
    )(pos, h, jnp.zeros((P, D), h.dtype))


def _combine_kernel(pos_ref, w_ref, y_ref, x_ref, gate_ref, g_ref, sc_ref, sh_ref, xo_ref, h_ref, buf_ref, sem):
    tm = x_ref.shape[0]

    def row_copy(r, k):
        return pltpu.make_async_copy(y_ref.at[pl.ds(pos_ref[k, r], 1), :], buf_ref.at[k, pl.ds(r, 1), :], sem)

    def start(r, c):
        for k in range(TOP_K):
            row_copy(r, k).start()
        return c

    def wait(r, c):
        for k in range(TOP_K):
            row_copy(r, k).wait()
        return c

    lax.fori_loop(0, tm, start, 0)
    lax.fori_loop(0, tm, wait, 0)
    w = w_ref[...]
    moe = w[:, 0:1] * buf_ref[0] + w[:, 1:2] * buf_ref[1]
    xn = x_ref[...] + gate_ref[...] * moe
    xo_ref[...] = xn
    h_ref[...] = _norm_mod(xn, g_ref[...], sc_ref[...], sh_ref[...]).astype(h_ref.dtype)


def _combine(pos, w_col, y, x2, gate, g, scale, shift, S, h_dtype):
    N, D = x2.shape
    tm = _pick(S, 256)
    row = pl.BlockSpec((tm, D), lambda i: (i, 0))
    return pl.pallas_call(
        _combine_kernel,
        grid=(N // tm,),
        in_specs=[
            pl.BlockSpec((8, tm), lambda i: (0, i), memory_space=pltpu.SMEM),
            pl.BlockSpec((tm, TOP_K), lambda i: (i, 0)),
            pl.BlockSpec(memory_space=pl.ANY),
            row,
            _row_vec_spec(D, tm, S),
            pl.BlockSpec((1, D), lambda i: (0, 0)),
            _row_vec_spec(D, tm, S),
            _row_vec_spec(D, tm, S),
        ],
        out_specs=[row, row],
        out_shape=[jax.ShapeDtypeStruct((N, D), F32), jax.ShapeDtypeStruct((N, D), h_dtype)],
        scratch_shapes=[pltpu.VMEM((TOP_K, tm, D), F32), pltpu.SemaphoreType.DMA(())],
        compiler_params=_params(("arbitrary",), 40),
        name="moe_combine",
    )(pos, w_col, y, x2, gate, g, scale, shift)


def _split_hi_lo(w):
    hi = w.astype(BF16)
    lo = (w - hi.astype(F32)).astype(BF16)
    return hi, lo


def kernel(x, c, rel_bias, ada_w, ada_b, norm_g, a_wqkv, a_q_norm, a_k_norm, a_sinks, a_wo, b_wdown, b_q_a_norm, b_kv_a_norm, b_wuq, b_wukv, b_q_norm, b_k_norm, b_wo, ffn_wg, ffn_wu, ffn_wd, moe_router, moe_router_b, moe_wg, moe_wu, moe_wd):
    B, S, D = x.shape
    N = B * S
    depth = ada_w.shape[0]
    n_mod = ada_w.shape[2] // D
    Dh = a_q_norm.shape[-1]
    H = a_sinks.shape[-1]
    KV = (a_wqkv.shape[-1] // Dh - H) // 2
    G = H // KV
    QL = b_q_a_norm.shape[-1]
    KVL = b_kv_a_norm.shape[-1]
    R = b_wdown.shape[-1] - QL - KVL
    NOPE = b_q_norm.shape[-1] - R
    HB = b_wuq.shape[-1] // (NOPE + R)
    V = b_wukv.shape[-1] // HB - NOPE
    E = moe_router.shape[-1]

    rows = ((B + 15) // 16) * 16
    c_pad = jnp.zeros((rows, D), F32).at[:B].set(c)
    mod = _adaln(c_pad, ada_w, ada_b)[:, :B].reshape(depth, B, n_mod, D)
    mod_vec = lambda layer, j: mod[layer, :, j].reshape(B, 1, D)
    gain = lambda layer, j: norm_g[layer, j].reshape(1, D)

    bias_tab = _swa_bias_tables(rel_bias, KV, G)
    pos_f = jnp.arange(S, dtype=F32)
    inv = ROPE_THETA ** (-jnp.arange(0, R, 2, dtype=F32) / R)
    ang = pos_f[:, None] * inv[None, :]
    cos, sin = jnp.cos(ang), jnp.sin(ang)
    cos2 = jnp.concatenate([cos, cos], axis=-1)
    sin2 = jnp.concatenate([-sin, sin], axis=-1)
    swap = lambda t: jnp.concatenate([t[..., R // 2:], t[..., :R // 2]], axis=-1)

    x2 = x.reshape(N, D)
    h = _normmod(x2, gain(0, 0), mod_vec(0, 1), mod_vec(0, 0), S, BF16)
    one_tile = jnp.zeros((N,), jnp.int32)

    for layer in range(depth):
        i = layer // 2
        moe_layer = layer % 2 == 1
        h2_dtype = F32 if moe_layer else BF16
        if not moe_layer:
            qkv = _matmul(h, a_wqkv[i].astype(BF16))
            o = _swa(qkv, bias_tab, a_q_norm[i], a_k_norm[i], a_sinks[i], B, S, H, KV, Dh)
            wo = a_wo[i]
        else:
            wd_ext = jnp.concatenate([b_wdown[i], swap(b_wdown[i][:, QL + KVL:])], axis=1).astype(BF16)
            kg = b_k_norm[i]
            cq, ckv, kpe = _mla_down(h, wd_ext, b_q_a_norm[i], b_kv_a_norm[i], kg[NOPE:], swap(kg[NOPE:]),
                                     cos2, sin2, S, QL, KVL, R)
            wq = b_wuq[i].reshape(QL, HB, NOPE + R)
            wq_h = jnp.concatenate([wq, swap(wq[..., NOPE:])], axis=-1).transpose(1, 0, 2).astype(BF16)
            wkv_h = b_wukv[i].reshape(KVL, HB, NOPE + V).transpose(1, 0, 2).astype(BF16)
            qg = b_q_norm[i]
            qf, kf, vf = _mla_up(cq, ckv, kpe, cos2, sin2, wq_h, wkv_h, qg[:NOPE], qg[NOPE:], swap(qg[NOPE:]),
                                 kg[:NOPE], B, S, NOPE, R, V)
            o = _flash(qf, kf, vf)
            wo = b_wo[i]
        x2, h2 = _proj_res(o, wo.astype(BF16), x2, mod_vec(layer, 2), gain(layer, 1), mod_vec(layer, 4),
                           mod_vec(layer, 3), S, h2_dtype)

        nxt = min(layer + 1, depth - 1)
        nxt_args = (gain(nxt, 0), mod_vec(nxt, 1), mod_vec(nxt, 0), S, BF16)
        if not moe_layer:
            tm = _pick(S, 512)
            x2, h = _ffn(h2, ffn_wg[i][None].astype(BF16), ffn_wu[i][None].astype(BF16),
                         ffn_wd[i][None].astype(BF16), one_tile[:N // tm], jnp.full((1,), N // tm, jnp.int32), tm,
                         epilogue=(x2, mod_vec(layer, 5)) + nxt_args)
        else:
            T = _pick(N // E, 512)
            n_tiles = (N * TOP_K) // T + E
            w_hi, w_lo = _split_hi_lo(moe_router[i].T)
            meta_i, meta_w, counts = _router(h2, jnp.concatenate([w_hi, w_lo], axis=0), moe_router_b[i])
            pos, tile_tab = _positions(meta_i, counts, T, n_tiles)
            xs = _dispatch(pos, h2, n_tiles * T)
            y = _ffn(xs, moe_wg[i].astype(BF16), moe_wu[i].astype(BF16), moe_wd[i].astype(BF16),
                     tile_tab[0, :n_tiles], tile_tab[1, :1], T)
            x2, h = _combine(pos, meta_w[:TOP_K].T, y, x2, mod_vec(layer, 5), *nxt_args)
    return x2.reshape(B, S, D)
```

```python
import functools
import math

import jax
import jax.numpy as jnp
from jax import lax
from jax.experimental import pallas as pl
from jax.experimental.pallas import tpu as pltpu

EPS = 1e-6
NEG_INF = -1e30
WINDOW = 128
BLOCK = 128
REL_MAX_DIST = 128
ROPE_THETA = 10000.0
TOP_K = 2
LOG2E = math.log2(math.e)

F32 = jnp.float32
BF16 = jnp.bfloat16
MIB = 1024 * 1024

_NT = (((1,), (1,)), ((), ()))


def _params(sem, vmem_mib, flags=None):
    return pltpu.CompilerParams(dimension_semantics=sem, vmem_limit_bytes=vmem_mib * MIB, flags=flags)


def _pick(n, pref):
    t = min(pref, n)
    while n % t:
        t //= 2
    return t


def _rms(x):
    return x * lax.rsqrt(jnp.mean(x * x, axis=-1, keepdims=True) + EPS)


def _norm_mod(x, g, scale, shift):
    return (_rms(x) * g) * (1.0 + scale) + shift


def _silu(x):
    return x * (1.0 / (1.0 + jnp.exp(-x)))


def _resident(shape, index_map):
    return pl.BlockSpec(shape, index_map, pipeline_mode=pl.Buffered(1))


def _adaln_kernel(c_ref, w_ref, b_ref, o_ref):
    ca = _silu(c_ref[...]).astype(BF16)
    o_ref[...] = jnp.dot(ca, w_ref[...].astype(BF16), preferred_element_type=F32) + b_ref[...]


def _adaln(c_pad, ada_w, ada_b):
    L, D, N6 = ada_w.shape
    R = c_pad.shape[0]
    tn = _pick(N6, 1536)
    return pl.pallas_call(
        _adaln_kernel,
        grid=(L, N6 // tn),
        in_specs=[
            pl.BlockSpec((R, D), lambda l, j: (0, 0)),
            pl.BlockSpec((None, D, tn), lambda l, j: (l, 0, j)),
            pl.BlockSpec((None, 1, tn), lambda l, j: (l, 0, j)),
        ],
        out_specs=pl.BlockSpec((None, R, tn), lambda l, j: (l, 0, j)),
        out_shape=jax.ShapeDtypeStruct((L, R, N6), F32),
        compiler_params=_params(("arbitrary", "arbitrary"), 48),
        name="adaln",
    )(c_pad, ada_w, ada_b.reshape(L, 1, N6))


def _row_vec_spec(D, tm, S):
    return pl.BlockSpec((None, 1, D), lambda i: ((i * tm) // S, 0, 0))


def _matmul_kernel(a_ref, w_ref, *refs, tn, norm):
    if norm:
        g_ref, sc_ref, sh_ref, o_ref = refs
        a = _norm_mod(a_ref[...], g_ref[...], sc_ref[...], sh_ref[...]).astype(BF16)
    else:
        (o_ref,) = refs
        a = a_ref[...]
    for j in range(0, o_ref.shape[-1], tn):
        o_ref[:, j:j + tn] = jnp.dot(a, w_ref[:, j:j + tn], preferred_element_type=F32).astype(o_ref.dtype)


def _matmul(a, w, S, norm=None, out_dtype=BF16):
    N, K = a.shape
    Nout = w.shape[1]
    tm = _pick(S, 512)
    tn = _pick(Nout, 512)
    in_specs = [pl.BlockSpec((tm, K), lambda i: (i, 0)), _resident((K, Nout), lambda i: (0, 0))]
    args = [a, w]
    if norm is not None:
        in_specs += [pl.BlockSpec((1, K), lambda i: (0, 0)), _row_vec_spec(K, tm, S), _row_vec_spec(K, tm, S)]
        args += list(norm)
    return pl.pallas_call(
        functools.partial(_matmul_kernel, tn=tn, norm=norm is not None),
        grid=(N // tm,),
        in_specs=in_specs,
        out_specs=pl.BlockSpec((tm, Nout), lambda i: (i, 0)),
        out_shape=jax.ShapeDtypeStruct((N, Nout), out_dtype),
        compiler_params=_params(("arbitrary",), 48),
        name="matmul",
    )(*args)


def _t5_bucket(n, n_buckets):
    max_exact = n_buckets // 2
    nf = jnp.maximum(n, 1).astype(F32)
    large = max_exact + (jnp.log(nf / max_exact) / math.log(REL_MAX_DIST / max_exact)
                         * (n_buckets - max_exact)).astype(jnp.int32)
    large = jnp.minimum(large, n_buckets - 1)
    return jnp.where(n < max_exact, n, large)


def _swa_bias_tables(rel_bias, KV, G):
    t_idx = jnp.arange(BLOCK)[:, None]
    j_idx = jnp.arange(2 * BLOCK)[None, :]
    dist = t_idx + BLOCK - j_idx
    in_window = (dist >= 0) & (dist < WINDOW)
    bias = rel_bias[_t5_bucket(jnp.maximum(dist, 0), rel_bias.shape[0])].astype(F32)
    bias = bias.transpose(2, 0, 1) * LOG2E
    general = jnp.where(in_window[None], bias, NEG_INF)
    first = jnp.where((in_window & (j_idx >= BLOCK))[None], bias, NEG_INF)
    tabs = jnp.stack([first, general]).reshape(2, KV, G, BLOCK, 2 * BLOCK)
    return tabs.transpose(0, 1, 4, 2, 3).reshape(2, KV, 2 * BLOCK, G * BLOCK)


def _swa_kernel(q_ref, kp_ref, kc_ref, vp_ref, vc_ref, bias_ref, qg_ref, kg_ref, sink_ref, o_ref, *, KV, G, Dh):
    T = BLOCK
    qt = q_ref[...].astype(F32).T
    k_all = jnp.concatenate([kp_ref[...], kc_ref[...]], axis=0).astype(F32)
    vt = jnp.concatenate([vp_ref[...], vc_ref[...]], axis=0).astype(F32).T
    qts = []
    for h in range(KV * G):
        x = qt[h * Dh:(h + 1) * Dh, :]
        r = lax.rsqrt(jnp.mean(x * x, axis=0, keepdims=True) + EPS)
        qts.append(((x * r) * qg_ref[...]).astype(BF16))
    scores = []
    for kv in range(KV):
        kh = (_rms(k_all[:, kv * Dh:(kv + 1) * Dh]) * kg_ref[...]).astype(BF16)
        qs_t = jnp.concatenate(qts[kv * G:(kv + 1) * G], axis=1)
        scores.append(jnp.dot(kh, qs_t, preferred_element_type=F32))
    probs = []
    for kv in range(KV):
        s = scores[kv] + bias_ref[kv]
        sink = sink_ref[kv]
        m = jnp.maximum(jnp.max(s, axis=0, keepdims=True), sink)
        p = jnp.exp2(s - m)
        denom = jnp.sum(p, axis=0, keepdims=True) + jnp.exp2(sink - m)
        probs.append((p.astype(BF16), 1.0 / denom))
    outs_t = []
    for kv in range(KV):
        p, inv = probs[kv]
        o_t = jnp.dot(vt[kv * Dh:(kv + 1) * Dh, :].astype(BF16), p, preferred_element_type=F32) * inv
        outs_t += [o_t[:, g * T:(g + 1) * T] for g in range(G)]
    o_ref[...] = jnp.concatenate(outs_t, axis=0).T.astype(o_ref.dtype)


def _swa(qkv, bias_tab, q_g, k_g, sinks, B, S, H, KV, Dh):
    G = H // KV
    nb = S // BLOCK
    HD, KD = H * Dh, KV * Dh
    assert HD % KD == 0
    kcol = HD // KD
    qkv3 = qkv.reshape(B, S, HD + 2 * KD)
    sink_row = jnp.broadcast_to((sinks.astype(F32) * LOG2E).reshape(KV, 1, G, 1),
                                (KV, 1, G, BLOCK)).reshape(KV, 1, G * BLOCK)
    q_gt = jnp.broadcast_to((q_g * (Dh ** -0.5 * LOG2E))[:, None], (Dh, BLOCK))
    prev = lambda b, i: jnp.maximum(i - 1, 0)
    const = lambda shape: pl.BlockSpec(shape, lambda b, i: (0,) * len(shape))
    out = pl.pallas_call(
        functools.partial(_swa_kernel, KV=KV, G=G, Dh=Dh),
        grid=(B, nb),
        in_specs=[
            pl.BlockSpec((None, BLOCK, HD), lambda b, i: (b, i, 0)),
            pl.BlockSpec((None, BLOCK, KD), lambda b, i: (b, prev(b, i), kcol)),
            pl.BlockSpec((None, BLOCK, KD), lambda b, i: (b, i, kcol)),
            pl.BlockSpec((None, BLOCK, KD), lambda b, i: (b, prev(b, i), kcol + 1)),
            pl.BlockSpec((None, BLOCK, KD), lambda b, i: (b, i, kcol + 1)),
            pl.BlockSpec((None, KV, 2 * BLOCK, G * BLOCK), lambda b, i: (jnp.minimum(i, 1), 0, 0, 0)),
            const((Dh, BLOCK)), const((1, Dh)), const((KV, 1, G * BLOCK)),
        ],
        out_specs=pl.BlockSpec((None, BLOCK, HD), lambda b, i: (b, i, 0)),
        out_shape=jax.ShapeDtypeStruct((B, S, HD), BF16),
        compiler_params=_params(("arbitrary", "arbitrary"), 48),
        name="swa",
    )(qkv3, qkv3, qkv3, qkv3, qkv3, bias_tab, q_gt, k_g.reshape(1, Dh), sink_row)
    return out.reshape(B * S, HD)


def _down_kernel(a_ref, w_ref, qag_ref, kvag_ref, kg_ref, kgs_ref, cos_ref, sin_ref,
                 cq_ref, ckv_ref, kpe_ref, *, QL, KVL, R):
    lat = jnp.dot(a_ref[...], w_ref[...], preferred_element_type=F32)
    cq_ref[...] = (_rms(lat[:, :QL]) * qag_ref[...]).astype(cq_ref.dtype)
    ckv_ref[...] = (_rms(lat[:, QL:QL + KVL]) * kvag_ref[...]).astype(ckv_ref.dtype)
    kpe = lat[:, QL + KVL:QL + KVL + R]
    kpe_sw = lat[:, QL + KVL + R:]
    r = lax.rsqrt(jnp.mean(kpe * kpe, axis=-1, keepdims=True) + EPS)
    kpe_ref[...] = (kpe * r * kg_ref[...]) * cos_ref[...] + (kpe_sw * r * kgs_ref[...]) * sin_ref[...]


def _mla_down(h, w_ext, qa_g, kva_g, kg_pe, kg_pe_sw, cos2, sin2, S, QL, KVL, R):
    N, D = h.shape
    tm = _pick(S, 512)
    W = w_ext.shape[1]
    ns = S // tm
    vec = lambda n: pl.BlockSpec((1, n), lambda i: (0, 0))
    pos_spec = pl.BlockSpec((tm, R), lambda i: (i % ns, 0))
    return pl.pallas_call(
        functools.partial(_down_kernel, QL=QL, KVL=KVL, R=R),
        grid=(N // tm,),
        in_specs=[pl.BlockSpec((tm, D), lambda i: (i, 0)), _resident((D, W), lambda i: (0, 0)),
                  vec(QL), vec(KVL), vec(R), vec(R), pos_spec, pos_spec],
        out_specs=[pl.BlockSpec((tm, QL), lambda i: (i, 0)), pl.BlockSpec((tm, KVL), lambda i: (i, 0)),
                   pl.BlockSpec((tm, R), lambda i: (i, 0))],
        out_shape=[jax.ShapeDtypeStruct((N, QL), BF16), jax.ShapeDtypeStruct((N, KVL), BF16),
                   jax.ShapeDtypeStruct((N, R), F32)],
        compiler_params=_params(("arbitrary",), 40),
        name="mla_down",
    )(h, w_ext, qa_g.reshape(1, QL), kva_g.reshape(1, KVL), kg_pe.reshape(1, R), kg_pe_sw.reshape(1, R), cos2, sin2)


def _up_kernel(cq_ref, ckv_ref, kpe_ref, cos_ref, sin_ref, wq_ref, wkv_ref, qgn_ref, qgp_ref, qgps_ref, kgn_ref,
               q_ref, k_ref, v_ref, *, NOPE, R):
    HP = q_ref.shape[0]
    h0 = pl.program_id(1) * HP
    cq, ckv = cq_ref[...], ckv_ref[...]
    qhs = [jnp.dot(cq, wq_ref[h0 + j], preferred_element_type=F32) for j in range(HP)]
    kvhs = [jnp.dot(ckv, wkv_ref[h0 + j], preferred_element_type=F32) for j in range(HP)]
    kpe = kpe_ref[...].astype(k_ref.dtype)
    tk = v_ref.shape[-1]
    for j in range(HP):
        qh, kvh = qhs[j], kvhs[j]
        qn, qp, qps = qh[:, :NOPE], qh[:, NOPE:NOPE + R], qh[:, NOPE + R:]
        q_ref[j, :, :NOPE] = (_rms(qn) * qgn_ref[...]).astype(q_ref.dtype)
        rp = lax.rsqrt(jnp.mean(qp * qp, axis=-1, keepdims=True) + EPS)
        q_rot = (qp * rp * qgp_ref[...]) * cos_ref[...] + (qps * rp * qgps_ref[...]) * sin_ref[...]
        q_ref[j, :, NOPE:] = q_rot.astype(q_ref.dtype)
        k_ref[j, :, :NOPE] = (_rms(kvh[:, :NOPE]) * kgn_ref[...]).astype(k_ref.dtype)
        k_ref[j, :, NOPE:] = kpe
        for c in range(v_ref.shape[1]):
            v_ref[j, c] = kvh[c * tk:(c + 1) * tk, NOPE:].T.astype(v_ref.dtype)


def _mla_up(cq, ckv, kpe, cos2, sin2, wq_h, wkv_h, qg_n, qg_p, qg_ps, kg_n, B, S, NOPE, R, V, tk):
    N, QL = cq.shape
    KVL = ckv.shape[1]
    H = wq_h.shape[0]
    HP = _pick(H, 4)
    tm = _pick(S, 512)
    ns = S // tm
    nc = tm // tk
    scale = (NOPE + R) ** -0.5 * LOG2E
    row = lambda n: pl.BlockSpec((tm, n), lambda i, h: (i, 0))
    pos_spec = pl.BlockSpec((tm, R), lambda i, h: (i % ns, 0))
    vec = lambda n: pl.BlockSpec((1, n), lambda i, h: (0, 0))
    head_out = lambda n: pl.BlockSpec((None, HP, tm, n), lambda i, h: (i // ns, h, i % ns, 0))
    return pl.pallas_call(
        functools.partial(_up_kernel, NOPE=NOPE, R=R),
        grid=(N // tm, H // HP),
        in_specs=[row(QL), row(KVL), row(R), pos_spec, pos_spec,
                  _resident(wq_h.shape, lambda i, h: (0, 0, 0)), _resident(wkv_h.shape, lambda i, h: (0, 0, 0)),
                  vec(NOPE), vec(R), vec(R), vec(NOPE)],
        out_specs=[head_out(NOPE + R), head_out(NOPE + R),
                   pl.BlockSpec((None, HP, nc, V, tk), lambda i, h: (i // ns, h, i % ns, 0, 0))],
        out_shape=[jax.ShapeDtypeStruct((B, H, S, NOPE + R), BF16), jax.ShapeDtypeStruct((B, H, S, NOPE + R), BF16),
                   jax.ShapeDtypeStruct((B, H, S // tk, V, tk), BF16)],
        compiler_params=_params(("arbitrary", "arbitrary"), 48),
        name="mla_up",
    )(cq, ckv, kpe, cos2, sin2, wq_h, wkv_h, (qg_n * scale).reshape(1, NOPE), (qg_p * scale).reshape(1, R),
      (qg_ps * scale).reshape(1, R), kg_n.reshape(1, NOPE))


def _flash_kernel(q_ref, k_ref, vt_ref, o_ref, sa_ref, sb_ref, *, tq):
    qi = pl.program_id(2)
    HP, V, tk = vt_ref.shape[0], vt_ref.shape[2], vt_ref.shape[3]
    assert tk == tq

    def qk(ki, dst_ref):
        start = pl.multiple_of(ki * tk, tk)
        for hh in range(HP):
            dst_ref[hh] = lax.dot_general(k_ref[hh, pl.ds(start, tk), :], q_ref[hh], _NT,
                                          preferred_element_type=F32)

    def consume(src_ref, ki, carry, diagonal):
        probs = []
        for hh in range(HP):
            m, l, _ = carry[hh]
            s = src_ref[hh]
            if diagonal:
                kpos = lax.broadcasted_iota(jnp.int32, s.shape, 0)
                qpos = lax.broadcasted_iota(jnp.int32, s.shape, 1)
                s = jnp.where(kpos <= qpos, s, NEG_INF)
            m_new = jnp.maximum(m, jnp.max(s, axis=0, keepdims=True))
            alpha = jnp.exp2(m - m_new)
            p = jnp.exp2(s - m_new)
            probs.append((m_new, alpha, alpha * l + jnp.sum(p, axis=0, keepdims=True), p.astype(BF16)))
        out = []
        for hh in range(HP):
            m_new, alpha, l, p = probs[hh]
            acc = alpha * carry[hh][2] + jnp.dot(vt_ref[hh, ki], p, preferred_element_type=F32)
            out.append((m_new, l, acc))
        return tuple(out)

    def pair(kk, carry):
        k0 = 2 * kk
        qk(k0 + 1, sb_ref)
        carry = consume(sa_ref, k0, carry, False)
        qk(k0 + 2, sa_ref)
        return consume(sb_ref, k0 + 1, carry, False)

    def last_even(carry):
        return consume(sa_ref, qi, carry, True)

    def last_odd(carry):
        qk(qi, sb_ref)
        return consume(sb_ref, qi, consume(sa_ref, qi - 1, carry, False), True)

    init = (jnp.full((1, tq), NEG_INF, F32), jnp.zeros((1, tq), F32), jnp.zeros((V, tq), F32))
    qk(0, sa_ref)
    carry = lax.fori_loop(0, qi >> 1, pair, (init,) * HP)
    carry = lax.cond((qi & 1) == 0, last_even, last_odd, carry)
    for hh in range(HP):
        _, l, acc = carry[hh]
        o_ref[:, hh * V:(hh + 1) * V] = (acc * (1.0 / l)).T.astype(o_ref.dtype)


def _flash(q, k, vt, heads_per_step=4):
    B, H, S, DK = q.shape
    _, _, nk, V, tk = vt.shape
    tq = _pick(S, 512)
    HP = _pick(H, heads_per_step)
    out = pl.pallas_call(
        functools.partial(_flash_kernel, tq=tq),
        grid=(B, H // HP, S // tq),
        in_specs=[
            pl.BlockSpec((None, HP, tq, DK), lambda b, h, i: (b, h, i, 0)),
            pl.BlockSpec((None, HP, S, DK), lambda b, h, i: (b, h, 0, 0)),
            pl.BlockSpec((None, HP, nk, V, tk), lambda b, h, i: (b, h, 0, 0, 0)),
        ],
        out_specs=pl.BlockSpec((None, tq, HP * V), lambda b, h, i: (b, i, h)),
        out_shape=jax.ShapeDtypeStruct((B, S, H * V), BF16),
        scratch_shapes=[pltpu.VMEM((HP, tk, tq), F32), pltpu.VMEM((HP, tk, tq), F32)],
        compiler_params=_params(("arbitrary", "arbitrary", "arbitrary"), 56),
        name="mla_flash",
    )(q, k, vt)
    return out.reshape(B * S, H * V)


def _proj_res_kernel(a_ref, w_ref, x_ref, gate_ref, g_ref, sc_ref, sh_ref, xo_ref, h_ref):
    y = jnp.dot(a_ref[...], w_ref[...], preferred_element_type=F32)
    xn = x_ref[...] + gate_ref[...] * y
    xo_ref[...] = xn
    h_ref[...] = _norm_mod(xn, g_ref[...], sc_ref[...], sh_ref[...]).astype(h_ref.dtype)


def _proj_res(a, w, x2, gate, g, scale, shift, S, h_dtype):
    N, K = a.shape
    D = w.shape[1]
    tm = _pick(S, 512)
    return pl.pallas_call(
        _proj_res_kernel,
        grid=(N // tm,),
        in_specs=[
            pl.BlockSpec((tm, K), lambda i: (i, 0)),
            _resident((K, D), lambda i: (0, 0)),
            pl.BlockSpec((tm, D), lambda i: (i, 0)),
            _row_vec_spec(D, tm, S),
            pl.BlockSpec((1, D), lambda i: (0, 0)),
            _row_vec_spec(D, tm, S),
            _row_vec_spec(D, tm, S),
        ],
        out_specs=[pl.BlockSpec((tm, D), lambda i: (i, 0)), pl.BlockSpec((tm, D), lambda i: (i, 0))],
        out_shape=[jax.ShapeDtypeStruct((N, D), F32), jax.ShapeDtypeStruct((N, D), h_dtype)],
        compiler_params=_params(("arbitrary",), 48),
        name="proj_res",
    )(a, w, x2, gate, g, scale, shift)


def _ffn_kernel(te_ref, nu_ref, *refs, residual, nf, nsplit):
    if residual:
        xs_ref, wg_ref, wu_ref, wd_ref, x_ref, gate_ref, g_ref, sc_ref, sh_ref, xo_ref, h_ref, acc_ref = refs
        xb_ref = xs_ref
    else:
        xs_ref, wg_ref, wu_ref, wd_ref, y_ref, acc_ref, xb_ref = refs
    m = pl.program_id(0)
    f = pl.program_id(1)
    active = m < nu_ref[0]
    rows = acc_ref.shape[0] // nsplit

    @pl.when(f == 0)
    def _():
        acc_ref[...] = jnp.zeros_like(acc_ref)
        if not residual:
            xb_ref[...] = xs_ref[...].astype(BF16)

    def swiglu(n):
        xb = xb_ref[:n, :]
        gt = jnp.dot(xb, wg_ref[...].astype(BF16), preferred_element_type=F32)
        up = jnp.dot(xb, wu_ref[...].astype(BF16), preferred_element_type=F32)
        a = (_silu(gt) * up).astype(BF16)
        acc_ref[:n, :] += jnp.dot(a, wd_ref[...].astype(BF16), preferred_element_type=F32)

    if nsplit == 1:
        pl.when(active)(lambda: swiglu(acc_ref.shape[0]))
    else:
        pl.when(active & (nu_ref[1 + m] > rows))(lambda: swiglu(acc_ref.shape[0]))
        pl.when(active & (nu_ref[1 + m] <= rows))(lambda: swiglu(rows))

    @pl.when(f == nf - 1)
    def _():
        if residual:
            xn = x_ref[...] + gate_ref[...] * acc_ref[...]
            xo_ref[...] = xn
            h_ref[...] = _norm_mod(xn, g_ref[...], sc_ref[...], sh_ref[...]).astype(h_ref.dtype)
        else:
            y_ref[...] = _pack_halves(acc_ref[...])


def _pack_halves(x):
    bits = lax.bitcast_convert_type(x.astype(BF16).astype(F32), jnp.uint32)
    half = x.shape[1] // 2
    return (bits[:, :half] >> 16) | (bits[:, half:] & jnp.uint32(0xFFFF0000))


def _unpack_halves(w):
    lo = lax.bitcast_convert_type(w << 16, F32)
    hi = lax.bitcast_convert_type(w & jnp.uint32(0xFFFF0000), F32)
    return jnp.concatenate([lo, hi], axis=1)


def _ffn(xs, wg, wu, wd, li, tile_expert, tile_use, tm, tf, nsplit=1, epilogue=None):
    P, D = xs.shape
    F = wg.shape[-1]
    nf = F // tf
    nm = P // tm
    residual = epilogue is not None

    def fidx(m, f, te, nu):
        return jnp.where(m < nu[0], f, nf - 1)

    in_specs = [
        pl.BlockSpec((tm, D), lambda m, f, te, nu: (jnp.minimum(m, nu[0] - 1), 0)),
        pl.BlockSpec((None, None, D, tf), lambda m, f, te, nu: (li, te[m], 0, fidx(m, f, te, nu))),
        pl.BlockSpec((None, None, D, tf), lambda m, f, te, nu: (li, te[m], 0, fidx(m, f, te, nu))),
        pl.BlockSpec((None, None, tf, D), lambda m, f, te, nu: (li, te[m], fidx(m, f, te, nu), 0)),
    ]
    args = [xs, wg, wu, wd]
    row = pl.BlockSpec((tm, D), lambda m, f, te, nu: (m, 0))
    if residual:
        x2, gate, g, scale, shift, S, h_dtype = epilogue
        bvec = pl.BlockSpec((None, 1, D), lambda m, f, te, nu: ((m * tm) // S, 0, 0))
        in_specs += [row, bvec, pl.BlockSpec((1, D), lambda m, f, te, nu: (0, 0)), bvec, bvec]
        args += [x2, gate, g, scale, shift]
        out_specs = [row, row]
        out_shape = [jax.ShapeDtypeStruct((P, D), F32), jax.ShapeDtypeStruct((P, D), h_dtype)]
    else:
        out_specs = pl.BlockSpec((tm, D // 2), lambda m, f, te, nu: (m, 0))
        out_shape = jax.ShapeDtypeStruct((P, D // 2), jnp.uint32)
    return pl.pallas_call(
        functools.partial(_ffn_kernel, residual=residual, nf=nf, nsplit=nsplit),
        grid_spec=pltpu.PrefetchScalarGridSpec(
            num_scalar_prefetch=2,
            grid=(nm, nf),
            in_specs=in_specs,
            out_specs=out_specs,
            scratch_shapes=[pltpu.VMEM((tm, D), F32)] + ([] if residual else [pltpu.VMEM((tm, D), BF16)]),
        ),
        out_shape=out_shape,
        compiler_params=_params(("arbitrary", "arbitrary"), 56),
        name="ffn_res" if residual else "ffn_grouped",
    )(tile_expert, tile_use, *args)


def _router_kernel(h_ref, w_ref, b_ref, mi_ref, mw_ref, cnt_ref, carry_ref, *, E):
    i = pl.program_id(0)
    tm = h_ref.shape[0]

    @pl.when(i == 0)
    def _():
        carry_ref[...] = jnp.zeros_like(carry_ref)

    hf = h_ref[...]
    h_hi = hf.astype(BF16)
    h_lo = (hf - h_hi.astype(F32)).astype(BF16)
    w = w_ref[...]
    l2 = lax.dot_general(w, h_hi, _NT, preferred_element_type=F32)
    l1 = lax.dot_general(w[:E], h_lo, _NT, preferred_element_type=F32)
    logits = (l2[:E] + l2[E:] + l1) + b_ref[...]
    eid = lax.broadcasted_iota(jnp.int32, (E, tm), 0)
    m1 = jnp.max(logits, axis=0, keepdims=True)
    i1 = jnp.min(jnp.where(logits == m1, eid, E), axis=0, keepdims=True)
    rest = jnp.where(eid == i1, -jnp.inf, logits)
    m2 = jnp.max(rest, axis=0, keepdims=True)
    i2 = jnp.min(jnp.where(rest == m2, eid, E), axis=0, keepdims=True)
    e2 = jnp.exp(m2 - m1)
    inv = 1.0 / (1.0 + e2)
    oh1 = eid == i1
    oh2 = eid == i2
    cnt = jnp.where(oh1 | oh2, 1.0, 0.0)
    r_i = lax.broadcasted_iota(jnp.int32, (tm, tm), 0)
    c_i = lax.broadcasted_iota(jnp.int32, (tm, tm), 1)
    tri = jnp.where(r_i <= c_i, 1.0, 0.0).astype(BF16)
    incl = jnp.dot(cnt.astype(BF16), tri, preferred_element_type=F32)
    rank = carry_ref[...] + incl - cnt
    r1 = jnp.sum(jnp.where(oh1, rank, 0.0), axis=0, keepdims=True)
    r2 = jnp.sum(jnp.where(oh2, rank, 0.0), axis=0, keepdims=True)
    carry = carry_ref[...] + jnp.sum(cnt, axis=1, keepdims=True)
    carry_ref[...] = carry
    zi = jnp.zeros((4, tm), jnp.int32)
    mi_ref[...] = jnp.concatenate([i1, i2, r1.astype(jnp.int32), r2.astype(jnp.int32), zi], axis=0)
    mw_ref[...] = jnp.concatenate([inv, e2 * inv, jnp.zeros((6, tm), F32)], axis=0)
    cnt_ref[...] = jnp.broadcast_to(carry.astype(jnp.int32), cnt_ref.shape)


def _router(h, w_hl, b):
    N, D = h.shape
    E = b.shape[0]
    tm = _pick(N, 512)
    return pl.pallas_call(
        functools.partial(_router_kernel, E=E),
        grid=(N // tm,),
        in_specs=[pl.BlockSpec((tm, D), lambda i: (i, 0)), pl.BlockSpec((2 * E, D), lambda i: (0, 0)),
                  pl.BlockSpec((E, 1), lambda i: (0, 0))],
        out_specs=[pl.BlockSpec((8, tm), lambda i: (0, i)), pl.BlockSpec((8, tm), lambda i: (0, i)),
                   pl.BlockSpec((E, 128), lambda i: (0, 0))],
        out_shape=[jax.ShapeDtypeStruct((8, N), jnp.int32), jax.ShapeDtypeStruct((8, N), F32),
                   jax.ShapeDtypeStruct((E, 128), jnp.int32)],
        scratch_shapes=[pltpu.VMEM((E, 1), F32)],
        compiler_params=_params(("arbitrary",), 32),
        name="router",
    )(h, w_hl, b.reshape(E, 1))


def _pos_kernel(mi_ref, cnt_ref, pos_ref, te_ref, *, E, T):
    shift = T.bit_length() - 1
    cnt = cnt_ref[:, 0:1]
    padded = ((cnt + (T - 1)) >> shift) << shift
    eid = lax.broadcasted_iota(jnp.int32, (E, 1), 0)
    off = jnp.zeros((E, 1), jnp.int32)
    for j in range(E):
        off = off + jnp.where(eid > j, padded[j:j + 1, :], 0)
    end = off + padded
    e1, e2 = mi_ref[0:1, :], mi_ref[1:2, :]
    o1 = jnp.zeros_like(e1)
    o2 = jnp.zeros_like(e2)
    for j in range(E):
        o1 = o1 + jnp.where(e1 == j, off[j:j + 1, :], 0)
        o2 = o2 + jnp.where(e2 == j, off[j:j + 1, :], 0)
    zeros = jnp.zeros((6, pos_ref.shape[1]), jnp.int32)
    pos_ref[...] = jnp.concatenate([o1 + mi_ref[2:3, :], o2 + mi_ref[3:4, :], zeros], axis=0)
    tstart = lax.broadcasted_iota(jnp.int32, (1, te_ref.shape[1]), 1) * T
    te = jnp.zeros_like(tstart)
    for j in range(E):
        te = te + jnp.where(end[j:j + 1, :] <= tstart, 1, 0)
    te = jnp.minimum(te, E - 1)
    n_used = jnp.broadcast_to(end[E - 1:E, :] >> shift, te.shape)
    lane = lax.broadcasted_iota(jnp.int32, te.shape, 1)
    pad_lo = jnp.zeros_like(te)
    seg_end = jnp.zeros_like(te)
    tok_end = jnp.zeros_like(te)
    for j in range(E):
        pad_lo = pad_lo + jnp.where(lane == j, off[j:j + 1, :] + cnt[j:j + 1, :], 0)
        seg_end = seg_end + jnp.where(lane == j, end[j:j + 1, :], 0)
        tok_end = tok_end + jnp.where(te == j, off[j:j + 1, :] + cnt[j:j + 1, :], 0)
    tile_rows = jnp.clip(tok_end - tstart, 0, T)
    te_ref[...] = jnp.concatenate([te, n_used, pad_lo, seg_end, tile_rows, jnp.zeros((3, te.shape[1]), jnp.int32)],
                                  axis=0)


def _positions(meta_i, counts, T, n_tiles):
    N = meta_i.shape[1]
    E = counts.shape[0]
    ntp = ((n_tiles + 127) // 128) * 128
    return pl.pallas_call(
        functools.partial(_pos_kernel, E=E, T=T),
        out_shape=[jax.ShapeDtypeStruct((8, N), jnp.int32), jax.ShapeDtypeStruct((8, ntp), jnp.int32)],
        compiler_params=pltpu.CompilerParams(vmem_limit_bytes=32 * MIB),
        name="moe_positions",
    )(meta_i, counts)


_DMA_UNROLL = 8


def _dispatch_kernel(pos_ref, tab_ref, h_ref, xs_ref, stage_ref, zero_ref, sem, zsem, *, E):
    i = pl.program_id(0)
    n = pl.num_programs(0)
    tm = h_ref.shape[0]
    slot = lax.rem(i, 2)

    def row_copy(s, r, k, dst_row):
        return pltpu.make_async_copy(stage_ref.at[s, pl.ds(r, 1), :], xs_ref.at[pl.ds(dst_row, 1), :], sem.at[s])

    def wait_slot(s):
        def body(r, c):
            for k in range(TOP_K):
                row_copy(s, r, k, 0).wait()
            return c
        lax.fori_loop(0, tm, body, 0, unroll=_DMA_UNROLL)

    @pl.when(i >= 2)
    def _():
        wait_slot(slot)

    stage_ref[slot] = h_ref[...]

    def start(r, c):
        for k in range(TOP_K):
            row_copy(slot, r, k, pos_ref[k, r]).start(priority=k)
        return c

    lax.fori_loop(0, tm, start, 0, unroll=_DMA_UNROLL)

    @pl.when(i == n - 1)
    def _():
        zero_ref[...] = jnp.zeros_like(zero_ref)

        def pad_rows(issue):
            for e in range(E):
                def body(r, c):
                    cp = pltpu.make_async_copy(zero_ref.at[pl.ds(0, 1), :], xs_ref.at[pl.ds(r, 1), :], zsem)
                    if issue:
                        cp.start()
                    else:
                        cp.wait()
                    return c
                lax.fori_loop(tab_ref[2, e], tab_ref[3, e], body, 0)

        pad_rows(True)

        @pl.when(n >= 2)
        def _():
            wait_slot(1 - slot)
        wait_slot(slot)
        pad_rows(False)

        stage_ref[0] = jnp.zeros((tm, stage_ref.shape[2]), stage_ref.dtype)
        seg_end = tab_ref[3, E - 1]
        n_tail = (xs_ref.shape[0] - seg_end) // tm

        def tail_tiles(issue):
            def body(j, c):
                row0 = pl.multiple_of(seg_end + j * tm, tm)
                cp = pltpu.make_async_copy(stage_ref.at[0], xs_ref.at[pl.ds(row0, tm), :], zsem)
                if issue:
                    cp.start()
                else:
                    cp.wait()
                return c
            lax.fori_loop(0, n_tail, body, 0)

        tail_tiles(True)
        tail_tiles(False)


def _dispatch(pos, tile_tab, h, T, n_tiles, E):
    N, D = h.shape
    tm = _pick(T, 256)
    assert N % tm == 0
    P = n_tiles * T
    return pl.pallas_call(
        functools.partial(_dispatch_kernel, E=E),
        grid=(N // tm,),
        in_specs=[
            pl.BlockSpec((8, tm), lambda i: (0, i), memory_space=pltpu.SMEM),
            pl.BlockSpec(memory_space=pltpu.SMEM),
            pl.BlockSpec((tm, D), lambda i: (i, 0)),
        ],
        out_specs=pl.BlockSpec(memory_space=pl.ANY),
        out_shape=jax.ShapeDtypeStruct((P, D), h.dtype),
        scratch_shapes=[pltpu.VMEM((2, tm, D), h.dtype), pltpu.VMEM((8, D), h.dtype),
                        pltpu.SemaphoreType.DMA((2,)), pltpu.SemaphoreType.DMA(())],
        compiler_params=_params(("arbitrary",), 32),
        name="moe_dispatch",
    )(pos, tile_tab, h)


def _combine_kernel(pos_ref, posn_ref, w_ref, y_ref, x_ref, gate_ref, g_ref, sc_ref, sh_ref, xo_ref, h_ref,
                    buf_ref, sem):
    i = pl.program_id(0)
    n = pl.num_programs(0)
    tm = x_ref.shape[0]
    slot = lax.rem(i, 2)

    def row_copy(s, r, k, src_row):
        return pltpu.make_async_copy(y_ref.at[pl.ds(src_row, 1), :], buf_ref.at[s, k, pl.ds(r, 1), :], sem.at[s])

    def gather(p_ref, s):
        def body(r, c):
            for k in range(TOP_K):
                row_copy(s, r, k, p_ref[k, r]).start(priority=k)
            return c
        lax.fori_loop(0, tm, body, 0, unroll=_DMA_UNROLL)

    @pl.when(i == 0)
    def _():
        gather(pos_ref, 0)

    @pl.when(i + 1 < n)
    def _():
        gather(posn_ref, 1 - slot)

    def wait(r, c):
        for k in range(TOP_K):
            row_copy(slot, r, k, 0).wait()
        return c

    lax.fori_loop(0, tm, wait, 0, unroll=_DMA_UNROLL)
    w = w_ref[...]
    moe = w[:, 0:1] * _unpack_halves(buf_ref[slot, 0]) + w[:, 1:2] * _unpack_halves(buf_ref[slot, 1])
    xn = x_ref[...] + gate_ref[...] * moe
    xo_ref[...] = xn
    h_ref[...] = _norm_mod(xn, g_ref[...], sc_ref[...], sh_ref[...]).astype(h_ref.dtype)


def _combine(pos, w_col, y, x2, gate, g, scale, shift, S, h_dtype):
    N, D = x2.shape
    tm = _pick(S, 256)
    row = pl.BlockSpec((tm, D), lambda i: (i, 0))
    last = N // tm - 1
    return pl.pallas_call(
        _combine_kernel,
        grid=(N // tm,),
        in_specs=[
            pl.BlockSpec((8, tm), lambda i: (0, i), memory_space=pltpu.SMEM),
            pl.BlockSpec((8, tm), lambda i: (0, jnp.minimum(i + 1, last)), memory_space=pltpu.SMEM),
            pl.BlockSpec((tm, TOP_K), lambda i: (i, 0)),
            pl.BlockSpec(memory_space=pl.ANY),
            row,
            _row_vec_spec(D, tm, S),
            pl.BlockSpec((1, D), lambda i: (0, 0)),
            _row_vec_spec(D, tm, S),
            _row_vec_spec(D, tm, S),
        ],
        out_specs=[row, row],
        out_shape=[jax.ShapeDtypeStruct((N, D), F32), jax.ShapeDtypeStruct((N, D), h_dtype)],
        scratch_shapes=[pltpu.VMEM((2, TOP_K, tm, D // 2), jnp.uint32), pltpu.SemaphoreType.DMA((2,))],
        compiler_params=_params(("arbitrary",), 48),
        name="moe_combine",
    )(pos, pos, w_col, y, x2, gate, g, scale, shift)


def _split_hi_lo(w):
    hi = w.astype(BF16)
    lo = (w - hi.astype(F32)).astype(BF16)
    return hi, lo


def kernel(x, c, rel_bias, ada_w, ada_b, norm_g, a_wqkv, a_q_norm, a_k_norm, a_sinks, a_wo, b_wdown, b_q_a_norm, b_kv_a_norm, b_wuq, b_wukv, b_q_norm, b_k_norm, b_wo, ffn_wg, ffn_wu, ffn_wd, moe_router, moe_router_b, moe_wg, moe_wu, moe_wd):
    B, S, D = x.shape
    N = B * S
    depth = ada_w.shape[0]
    n_mod = ada_w.shape[2] // D
    Dh = a_q_norm.shape[-1]
    H = a_sinks.shape[-1]
    KV = (a_wqkv.shape[-1] // Dh - H) // 2
    G = H // KV
    QL = b_q_a_norm.shape[-1]
    KVL = b_kv_a_norm.shape[-1]
    R = b_wdown.shape[-1] - QL - KVL
    NOPE = b_q_norm.shape[-1] - R
    HB = b_wuq.shape[-1] // (NOPE + R)
    V = b_wukv.shape[-1] // HB - NOPE
    E = moe_router.shape[-1]

    rows = ((B + 15) // 16) * 16
    c_pad = jnp.zeros((rows, D), F32).at[:B].set(c)
    mod = _adaln(c_pad, ada_w, ada_b)[:, :B].reshape(depth, B, n_mod, D)
    mod_vec = lambda layer, j: mod[layer, :, j].reshape(B, 1, D)
    gain = lambda layer, j: norm_g[layer, j].reshape(1, D)

    bias_tab = _swa_bias_tables(rel_bias, KV, G)
    pos_f = jnp.arange(S, dtype=F32)
    inv = ROPE_THETA ** (-jnp.arange(0, R, 2, dtype=F32) / R)
    ang = pos_f[:, None] * inv[None, :]
    cos, sin = jnp.cos(ang), jnp.sin(ang)
    cos2 = jnp.concatenate([cos, cos], axis=-1)
    sin2 = jnp.concatenate([-sin, sin], axis=-1)
    swap = lambda t: jnp.concatenate([t[..., R // 2:], t[..., :R // 2]], axis=-1)

    x2 = x.reshape(N, D)
    h = None
    one_tile = jnp.zeros((N,), jnp.int32)
    ffn_w = [w.astype(BF16)[:, None] for w in (ffn_wg, ffn_wu, ffn_wd)]
    moe_w = (moe_wg, moe_wu, moe_wd)
    F_moe = moe_wg.shape[-1]

    for layer in range(depth):
        i = layer // 2
        moe_layer = layer % 2 == 1
        h2_dtype = F32 if moe_layer else BF16
        if not moe_layer:
            if layer == 0:
                qkv = _matmul(x2, a_wqkv[i].astype(BF16), S, norm=(gain(0, 0), mod_vec(0, 1), mod_vec(0, 0)))
            else:
                qkv = _matmul(h, a_wqkv[i].astype(BF16), S)
            o = _swa(qkv, bias_tab, a_q_norm[i], a_k_norm[i], a_sinks[i], B, S, H, KV, Dh)
            wo = a_wo[i]
        else:
            wd_ext = jnp.concatenate([b_wdown[i], swap(b_wdown[i][:, QL + KVL:])], axis=1).astype(BF16)
            kg = b_k_norm[i]
            cq, ckv, kpe = _mla_down(h, wd_ext, b_q_a_norm[i], b_kv_a_norm[i], kg[NOPE:], swap(kg[NOPE:]),
                                     cos2, sin2, S, QL, KVL, R)
            wq = b_wuq[i].reshape(QL, HB, NOPE + R)
            wq_h = jnp.concatenate([wq, swap(wq[..., NOPE:])], axis=-1).transpose(1, 0, 2).astype(BF16)
            wkv_h = b_wukv[i].reshape(KVL, HB, NOPE + V).transpose(1, 0, 2).astype(BF16)
            qg = b_q_norm[i]
            qf, kf, vf = _mla_up(cq, ckv, kpe, cos2, sin2, wq_h, wkv_h, qg[:NOPE], qg[NOPE:], swap(qg[NOPE:]),
                                 kg[:NOPE], B, S, NOPE, R, V, _pick(S, 512))
            o = _flash(qf, kf, vf)
            wo = b_wo[i]
        x2, h2 = _proj_res(o, wo.astype(BF16), x2, mod_vec(layer, 2), gain(layer, 1), mod_vec(layer, 4),
                           mod_vec(layer, 3), S, h2_dtype)

        nxt = min(layer + 1, depth - 1)
        nxt_args = (gain(nxt, 0), mod_vec(nxt, 1), mod_vec(nxt, 0), S, BF16)
        if not moe_layer:
            tm = _pick(S, 512)
            all_tiles = jnp.concatenate([jnp.full((1,), N // tm, jnp.int32), jnp.full((N // tm,), tm, jnp.int32)])
            x2, h = _ffn(h2, *ffn_w, i, one_tile[:N // tm], all_tiles, tm,
                         _pick(ffn_wg.shape[-1], 512), epilogue=(x2, mod_vec(layer, 5)) + nxt_args)
        else:
            T = _pick(N // E, 1024)
            n_tiles = (N * TOP_K) // T + E
            w_hi, w_lo = _split_hi_lo(moe_router[i].T)
            meta_i, meta_w, counts = _router(h2, jnp.concatenate([w_hi, w_lo], axis=0), moe_router_b[i])
            pos, tile_tab = _positions(meta_i, counts, T, n_tiles)
            xs = _dispatch(pos, tile_tab, h2, T, n_tiles, E)
            tile_use = jnp.concatenate([tile_tab[1, :1], tile_tab[4, :n_tiles]])
            y = _ffn(xs, *moe_w, i, tile_tab[0, :n_tiles], tile_use, T, _pick(F_moe, 256),
                     nsplit=2 if T >= 256 else 1)
            x2, h = _combine(pos, meta_w[:TOP_K].T, y, x2, mod_vec(layer, 5), *nxt_args)
    return x2.reshape(B, S, D)
```

```python
import functools
import math

import jax
import jax.numpy as jnp
from jax import lax
from jax.experimental import pallas as pl
from jax.experimental.pallas import tpu as pltpu

EPS = 1e-6
NEG_INF = -1e30
WINDOW = 128
BLOCK = 128
REL_MAX_DIST = 128
ROPE_THETA = 10000.0
TOP_K = 2
LOG2E = math.log2(math.e)

F32 = jnp.float32
BF16 = jnp.bfloat16
MIB = 1024 * 1024

_NT = (((1,), (1,)), ((), ()))


def _params(sem, vmem_mib, flags=None):
    return pltpu.CompilerParams(dimension_semantics=sem, vmem_limit_bytes=vmem_mib * MIB, flags=flags)


def _pick(n, pref):
    t = min(pref, n)
    while n % t:
        t //= 2
    return t


def _rms(x):
    return x * lax.rsqrt(jnp.mean(x * x, axis=-1, keepdims=True) + EPS)


def _norm_mod(x, g, scale, shift):
    return (_rms(x) * g) * (1.0 + scale) + shift


def _silu(x):
    return x * (1.0 / (1.0 + jnp.exp(-x)))


def _resident(shape, index_map):
    return pl.BlockSpec(shape, index_map, pipeline_mode=pl.Buffered(1))


def _adaln_kernel(c_ref, w_ref, b_ref, o_ref):
    ca = _silu(c_ref[...]).astype(BF16)
    o_ref[...] = jnp.dot(ca, w_ref[...].astype(BF16), preferred_element_type=F32) + b_ref[...]


def _adaln(c_pad, ada_w, ada_b):
    L, D, N6 = ada_w.shape
    R = c_pad.shape[0]
    tn = _pick(N6, 1536)
    return pl.pallas_call(
        _adaln_kernel,
        grid=(L, N6 // tn),
        in_specs=[
            pl.BlockSpec((R, D), lambda l, j: (0, 0)),
            pl.BlockSpec((None, D, tn), lambda l, j: (l, 0, j)),
            pl.BlockSpec((None, 1, tn), lambda l, j: (l, 0, j)),
        ],
        out_specs=pl.BlockSpec((None, R, tn), lambda l, j: (l, 0, j)),
        out_shape=jax.ShapeDtypeStruct((L, R, N6), F32),
        compiler_params=_params(("arbitrary", "arbitrary"), 48),
        name="adaln",
    )(c_pad, ada_w, ada_b.reshape(L, 1, N6))


def _row_vec_spec(D, tm, S):
    return pl.BlockSpec((None, 1, D), lambda i: ((i * tm) // S, 0, 0))


def _matmul_kernel(a_ref, w_ref, *refs, tn, norm):
    if norm:
        g_ref, sc_ref, sh_ref, o_ref = refs
        a = _norm_mod(a_ref[...], g_ref[...], sc_ref[...], sh_ref[...]).astype(BF16)
    else:
        (o_ref,) = refs
        a = a_ref[...]
    for j in range(0, o_ref.shape[-1], tn):
        o_ref[:, j:j + tn] = jnp.dot(a, w_ref[:, j:j + tn], preferred_element_type=F32).astype(o_ref.dtype)


def _matmul(a, w, S, norm=None, out_dtype=BF16):
    N, K = a.shape
    Nout = w.shape[1]
    tm = _pick(S, 512)
    tn = _pick(Nout, 512)
    in_specs = [pl.BlockSpec((tm, K), lambda i: (i, 0)), _resident((K, Nout), lambda i: (0, 0))]
    args = [a, w]
    if norm is not None:
        in_specs += [pl.BlockSpec((1, K), lambda i: (0, 0)), _row_vec_spec(K, tm, S), _row_vec_spec(K, tm, S)]
        args += list(norm)
    return pl.pallas_call(
        functools.partial(_matmul_kernel, tn=tn, norm=norm is not None),
        grid=(N // tm,),
        in_specs=in_specs,
        out_specs=pl.BlockSpec((tm, Nout), lambda i: (i, 0)),
        out_shape=jax.ShapeDtypeStruct((N, Nout), out_dtype),
        compiler_params=_params(("arbitrary",), 48),
        name="matmul",
    )(*args)


def _t5_bucket(n, n_buckets):
    max_exact = n_buckets // 2
    nf = jnp.maximum(n, 1).astype(F32)
    large = max_exact + (jnp.log(nf / max_exact) / math.log(REL_MAX_DIST / max_exact)
                         * (n_buckets - max_exact)).astype(jnp.int32)
    large = jnp.minimum(large, n_buckets - 1)
    return jnp.where(n < max_exact, n, large)


def _swa_bias_tables(rel_bias, KV, G):
    t_idx = jnp.arange(BLOCK)[:, None]
    j_idx = jnp.arange(2 * BLOCK)[None, :]
    dist = t_idx + BLOCK - j_idx
    in_window = (dist >= 0) & (dist < WINDOW)
    n_buckets = rel_bias.shape[0]
    onehot = jax.nn.one_hot(_t5_bucket(jnp.maximum(dist, 0), n_buckets).reshape(-1), n_buckets, dtype=F32)
    bias = jnp.dot(onehot, rel_bias.astype(F32), precision=lax.Precision.HIGHEST)
    bias = bias.reshape(BLOCK, 2 * BLOCK, -1).transpose(2, 0, 1) * LOG2E
    general = jnp.where(in_window[None], bias, NEG_INF)
    first = jnp.where((in_window & (j_idx >= BLOCK))[None], bias, NEG_INF)
    tabs = jnp.stack([first, general]).reshape(2, KV, G, BLOCK, 2 * BLOCK)
    return tabs.transpose(0, 1, 4, 2, 3).reshape(2, KV, 2 * BLOCK, G * BLOCK)


def _swa_kernel(q_ref, kp_ref, kc_ref, vp_ref, vc_ref, bias_ref, qg_ref, kg_ref, sink_ref, o_ref, *, KV, G, Dh):
    T = BLOCK
    qt = q_ref[...].astype(F32).T
    k_all = jnp.concatenate([kp_ref[...], kc_ref[...]], axis=0).astype(F32)
    vt = jnp.concatenate([vp_ref[...], vc_ref[...]], axis=0).astype(F32).T
    qts = []
    for h in range(KV * G):
        x = qt[h * Dh:(h + 1) * Dh, :]
        r = lax.rsqrt(jnp.mean(x * x, axis=0, keepdims=True) + EPS)
        qts.append(((x * r) * qg_ref[...]).astype(BF16))
    scores = []
    for kv in range(KV):
        kh = (_rms(k_all[:, kv * Dh:(kv + 1) * Dh]) * kg_ref[...]).astype(BF16)
        qs_t = jnp.concatenate(qts[kv * G:(kv + 1) * G], axis=1)
        scores.append(jnp.dot(kh, qs_t, preferred_element_type=F32))
    probs = []
    for kv in range(KV):
        s = scores[kv] + bias_ref[kv]
        sink = sink_ref[kv]
        m = jnp.maximum(jnp.max(s, axis=0, keepdims=True), sink)
        p = jnp.exp2(s - m)
        denom = jnp.sum(p, axis=0, keepdims=True) + jnp.exp2(sink - m)
        probs.append((p.astype(BF16), 1.0 / denom))
    outs_t = []
    for kv in range(KV):
        p, inv = probs[kv]
        o_t = jnp.dot(vt[kv * Dh:(kv + 1) * Dh, :].astype(BF16), p, preferred_element_type=F32) * inv
        outs_t += [o_t[:, g * T:(g + 1) * T] for g in range(G)]
    o_ref[...] = jnp.concatenate(outs_t, axis=0).T.astype(o_ref.dtype)


def _swa(qkv, bias_tab, q_g, k_g, sinks, B, S, H, KV, Dh):
    G = H // KV
    nb = S // BLOCK
    HD, KD = H * Dh, KV * Dh
    assert HD % KD == 0
    kcol = HD // KD
    qkv3 = qkv.reshape(B, S, HD + 2 * KD)
    sink_row = jnp.broadcast_to((sinks.astype(F32) * LOG2E).reshape(KV, 1, G, 1),
                                (KV, 1, G, BLOCK)).reshape(KV, 1, G * BLOCK)
    q_gt = jnp.broadcast_to((q_g * (Dh ** -0.5 * LOG2E))[:, None], (Dh, BLOCK))
    prev = lambda b, i: jnp.maximum(i - 1, 0)
    const = lambda shape: pl.BlockSpec(shape, lambda b, i: (0,) * len(shape))
    out = pl.pallas_call(
        functools.partial(_swa_kernel, KV=KV, G=G, Dh=Dh),
        grid=(B, nb),
        in_specs=[
            pl.BlockSpec((None, BLOCK, HD), lambda b, i: (b, i, 0)),
            pl.BlockSpec((None, BLOCK, KD), lambda b, i: (b, prev(b, i), kcol)),
            pl.BlockSpec((None, BLOCK, KD), lambda b, i: (b, i, kcol)),
            pl.BlockSpec((None, BLOCK, KD), lambda b, i: (b, prev(b, i), kcol + 1)),
            pl.BlockSpec((None, BLOCK, KD), lambda b, i: (b, i, kcol + 1)),
            pl.BlockSpec((None, KV, 2 * BLOCK, G * BLOCK), lambda b, i: (jnp.minimum(i, 1), 0, 0, 0)),
            const((Dh, BLOCK)), const((1, Dh)), const((KV, 1, G * BLOCK)),
        ],
        out_specs=pl.BlockSpec((None, BLOCK, HD), lambda b, i: (b, i, 0)),
        out_shape=jax.ShapeDtypeStruct((B, S, HD), BF16),
        compiler_params=_params(("arbitrary", "arbitrary"), 48),
        name="swa",
    )(qkv3, qkv3, qkv3, qkv3, qkv3, bias_tab, q_gt, k_g.reshape(1, Dh), sink_row)
    return out.reshape(B * S, HD)


def _down_kernel(a_ref, w_ref, qag_ref, kvag_ref, kg_ref, kgs_ref, cos_ref, sin_ref,
                 cq_ref, ckv_ref, kpe_ref, *, QL, KVL, R):
    lat = jnp.dot(a_ref[...], w_ref[...], preferred_element_type=F32)
    cq_ref[...] = (_rms(lat[:, :QL]) * qag_ref[...]).astype(cq_ref.dtype)
    ckv_ref[...] = (_rms(lat[:, QL:QL + KVL]) * kvag_ref[...]).astype(ckv_ref.dtype)
    kpe = lat[:, QL + KVL:QL + KVL + R]
    kpe_sw = lat[:, QL + KVL + R:]
    r = lax.rsqrt(jnp.mean(kpe * kpe, axis=-1, keepdims=True) + EPS)
    kpe_ref[...] = (kpe * r * kg_ref[...]) * cos_ref[...] + (kpe_sw * r * kgs_ref[...]) * sin_ref[...]


def _mla_down(h, w_ext, qa_g, kva_g, kg_pe, kg_pe_sw, cos2, sin2, S, QL, KVL, R):
    N, D = h.shape
    tm = _pick(S, 512)
    W = w_ext.shape[1]
    ns = S // tm
    vec = lambda n: pl.BlockSpec((1, n), lambda i: (0, 0))
    pos_spec = pl.BlockSpec((tm, R), lambda i: (i % ns, 0))
    return pl.pallas_call(
        functools.partial(_down_kernel, QL=QL, KVL=KVL, R=R),
        grid=(N // tm,),
        in_specs=[pl.BlockSpec((tm, D), lambda i: (i, 0)), _resident((D, W), lambda i: (0, 0)),
                  vec(QL), vec(KVL), vec(R), vec(R), pos_spec, pos_spec],
        out_specs=[pl.BlockSpec((tm, QL), lambda i: (i, 0)), pl.BlockSpec((tm, KVL), lambda i: (i, 0)),
                   pl.BlockSpec((tm, R), lambda i: (i, 0))],
        out_shape=[jax.ShapeDtypeStruct((N, QL), BF16), jax.ShapeDtypeStruct((N, KVL), BF16),
                   jax.ShapeDtypeStruct((N, R), F32)],
        compiler_params=_params(("arbitrary",), 40),
        name="mla_down",
    )(h, w_ext, qa_g.reshape(1, QL), kva_g.reshape(1, KVL), kg_pe.reshape(1, R), kg_pe_sw.reshape(1, R), cos2, sin2)


def _up_kernel(cq_ref, ckv_ref, kpe_ref, cos_ref, sin_ref, wq_ref, wkv_ref, qgn_ref, qgp_ref, qgps_ref, kgn_ref,
               q_ref, k_ref, v_ref, *, NOPE, R):
    HP = q_ref.shape[0]
    h0 = pl.program_id(1) * HP
    cq, ckv = cq_ref[...], ckv_ref[...]
    qhs = [jnp.dot(cq, wq_ref[h0 + j], preferred_element_type=F32) for j in range(HP)]
    kvhs = [jnp.dot(ckv, wkv_ref[h0 + j], preferred_element_type=F32) for j in range(HP)]
    kpe = kpe_ref[...].astype(k_ref.dtype)
    tk = v_ref.shape[-1]
    for j in range(HP):
        qh, kvh = qhs[j], kvhs[j]
        qn, qp, qps = qh[:, :NOPE], qh[:, NOPE:NOPE + R], qh[:, NOPE + R:]
        q_ref[j, :, :NOPE] = (_rms(qn) * qgn_ref[...]).astype(q_ref.dtype)
        rp = lax.rsqrt(jnp.mean(qp * qp, axis=-1, keepdims=True) + EPS)
        q_rot = (qp * rp * qgp_ref[...]) * cos_ref[...] + (qps * rp * qgps_ref[...]) * sin_ref[...]
        q_ref[j, :, NOPE:] = q_rot.astype(q_ref.dtype)
        k_ref[j, :, :NOPE] = (_rms(kvh[:, :NOPE]) * kgn_ref[...]).astype(k_ref.dtype)
        k_ref[j, :, NOPE:] = kpe
        for c in range(v_ref.shape[1]):
            v_ref[j, c] = kvh[c * tk:(c + 1) * tk, NOPE:].T.astype(v_ref.dtype)


def _mla_up(cq, ckv, kpe, cos2, sin2, wq_h, wkv_h, qg_n, qg_p, qg_ps, kg_n, B, S, NOPE, R, V, tk):
    N, QL = cq.shape
    KVL = ckv.shape[1]
    H = wq_h.shape[0]
    HP = _pick(H, 8)
    tm = _pick(S, 512)
    ns = S // tm
    nc = tm // tk
    scale = (NOPE + R) ** -0.5 * LOG2E
    row = lambda n: pl.BlockSpec((tm, n), lambda i, h: (i, 0))
    pos_spec = pl.BlockSpec((tm, R), lambda i, h: (i % ns, 0))
    vec = lambda n: pl.BlockSpec((1, n), lambda i, h: (0, 0))
    head_out = lambda n: pl.BlockSpec((None, HP, tm, n), lambda i, h: (i // ns, h, i % ns, 0))
    return pl.pallas_call(
        functools.partial(_up_kernel, NOPE=NOPE, R=R),
        grid=(N // tm, H // HP),
        in_specs=[row(QL), row(KVL), row(R), pos_spec, pos_spec,
                  _resident(wq_h.shape, lambda i, h: (0, 0, 0)), _resident(wkv_h.shape, lambda i, h: (0, 0, 0)),
                  vec(NOPE), vec(R), vec(R), vec(NOPE)],
        out_specs=[head_out(NOPE + R), head_out(NOPE + R),
                   pl.BlockSpec((None, HP, nc, V, tk), lambda i, h: (i // ns, h, i % ns, 0, 0))],
        out_shape=[jax.ShapeDtypeStruct((B, H, S, NOPE + R), BF16), jax.ShapeDtypeStruct((B, H, S, NOPE + R), BF16),
                   jax.ShapeDtypeStruct((B, H, S // tk, V, tk), BF16)],
        compiler_params=_params(("arbitrary", "arbitrary"), 48),
        name="mla_up",
    )(cq, ckv, kpe, cos2, sin2, wq_h, wkv_h, (qg_n * scale).reshape(1, NOPE), (qg_p * scale).reshape(1, R),
      (qg_ps * scale).reshape(1, R), kg_n.reshape(1, NOPE))


def _flash_kernel(q_ref, k_ref, vt_ref, o_ref, sa_ref, sb_ref, *, tq):
    qi = pl.program_id(2)
    HP, V, tk = vt_ref.shape[0], vt_ref.shape[2], vt_ref.shape[3]
    assert tk == tq

    def qk(ki, dst_ref):
        start = pl.multiple_of(ki * tk, tk)
        for hh in range(HP):
            dst_ref[hh] = lax.dot_general(k_ref[hh, pl.ds(start, tk), :], q_ref[hh], _NT,
                                          preferred_element_type=F32)

    def consume(src_ref, ki, carry, diagonal):
        probs = []
        for hh in range(HP):
            m, l, _ = carry[hh]
            s = src_ref[hh]
            if diagonal:
                kpos = lax.broadcasted_iota(jnp.int32, s.shape, 0)
                qpos = lax.broadcasted_iota(jnp.int32, s.shape, 1)
                s = jnp.where(kpos <= qpos, s, NEG_INF)
            m_new = jnp.maximum(m, jnp.max(s, axis=0, keepdims=True))
            alpha = jnp.exp2(m - m_new)
            p = jnp.exp2(s - m_new)
            probs.append((m_new, alpha, alpha * l + jnp.sum(p, axis=0, keepdims=True), p.astype(BF16)))
        out = []
        for hh in range(HP):
            m_new, alpha, l, p = probs[hh]
            acc = alpha * carry[hh][2] + jnp.dot(vt_ref[hh, ki], p, preferred_element_type=F32)
            out.append((m_new, l, acc))
        return tuple(out)

    def pair(kk, carry):
        k0 = 2 * kk
        qk(k0 + 1, sb_ref)
        carry = consume(sa_ref, k0, carry, False)
        qk(k0 + 2, sa_ref)
        return consume(sb_ref, k0 + 1, carry, False)

    def last_even(carry):
        return consume(sa_ref, qi, carry, True)

    def last_odd(carry):
        qk(qi, sb_ref)
        return consume(sb_ref, qi, consume(sa_ref, qi - 1, carry, False), True)

    init = (jnp.full((1, tq), NEG_INF, F32), jnp.zeros((1, tq), F32), jnp.zeros((V, tq), F32))
    qk(0, sa_ref)
    carry = lax.fori_loop(0, qi >> 1, pair, (init,) * HP)
    carry = lax.cond((qi & 1) == 0, last_even, last_odd, carry)
    for hh in range(HP):
        _, l, acc = carry[hh]
        o_ref[:, hh * V:(hh + 1) * V] = (acc * (1.0 / l)).T.astype(o_ref.dtype)


def _flash(q, k, vt, heads_per_step=4):
    B, H, S, DK = q.shape
    _, _, nk, V, tk = vt.shape
    tq = _pick(S, 512)
    HP = _pick(H, heads_per_step)
    out = pl.pallas_call(
        functools.partial(_flash_kernel, tq=tq),
        grid=(B, H // HP, S // tq),
        in_specs=[
            pl.BlockSpec((None, HP, tq, DK), lambda b, h, i: (b, h, i, 0)),
            pl.BlockSpec((None, HP, S, DK), lambda b, h, i: (b, h, 0, 0)),
            pl.BlockSpec((None, HP, nk, V, tk), lambda b, h, i: (b, h, 0, 0, 0)),
        ],
        out_specs=pl.BlockSpec((None, tq, HP * V), lambda b, h, i: (b, i, h)),
        out_shape=jax.ShapeDtypeStruct((B, S, H * V), BF16),
        scratch_shapes=[pltpu.VMEM((HP, tk, tq), F32), pltpu.VMEM((HP, tk, tq), F32)],
        compiler_params=_params(("arbitrary", "arbitrary", "arbitrary"), 56),
        name="mla_flash",
    )(q, k, vt)
    return out.reshape(B * S, H * V)


def _proj_res_kernel(a_ref, w_ref, x_ref, gate_ref, g_ref, sc_ref, sh_ref, xo_ref, h_ref):
    y = jnp.dot(a_ref[...], w_ref[...], preferred_element_type=F32)
    xn = x_ref[...] + gate_ref[...] * y
    xo_ref[...] = xn
    h_ref[...] = _norm_mod(xn, g_ref[...], sc_ref[...], sh_ref[...]).astype(h_ref.dtype)


def _proj_res(a, w, x2, gate, g, scale, shift, S, h_dtype):
    N, K = a.shape
    D = w.shape[1]
    tm = _pick(S, 512)
    return pl.pallas_call(
        _proj_res_kernel,
        grid=(N // tm,),
        in_specs=[
            pl.BlockSpec((tm, K), lambda i: (i, 0)),
            _resident((K, D), lambda i: (0, 0)),
            pl.BlockSpec((tm, D), lambda i: (i, 0)),
            _row_vec_spec(D, tm, S),
            pl.BlockSpec((1, D), lambda i: (0, 0)),
            _row_vec_spec(D, tm, S),
            _row_vec_spec(D, tm, S),
        ],
        out_specs=[pl.BlockSpec((tm, D), lambda i: (i, 0)), pl.BlockSpec((tm, D), lambda i: (i, 0))],
        out_shape=[jax.ShapeDtypeStruct((N, D), F32), jax.ShapeDtypeStruct((N, D), h_dtype)],
        compiler_params=_params(("arbitrary",), 48),
        name="proj_res",
    )(a, w, x2, gate, g, scale, shift)


def _ffn_kernel(te_ref, nu_ref, *refs, residual, nf, nsplit):
    if residual:
        xs_ref, wg_ref, wu_ref, wd_ref, x_ref, gate_ref, g_ref, sc_ref, sh_ref, xo_ref, h_ref, acc_ref = refs
        xb_ref = xs_ref
    else:
        xs_ref, wg_ref, wu_ref, wd_ref, y_ref, acc_ref, xb_ref = refs
    m = pl.program_id(0)
    f = pl.program_id(1)
    active = m < nu_ref[0]
    rows = acc_ref.shape[0] // nsplit

    @pl.when(f == 0)
    def _():
        acc_ref[...] = jnp.zeros_like(acc_ref)
        if not residual:
            xb_ref[...] = xs_ref[...].astype(BF16)

    def swiglu(n):
        xb = xb_ref[:n, :]
        gt = jnp.dot(xb, wg_ref[...].astype(BF16), preferred_element_type=F32)
        up = jnp.dot(xb, wu_ref[...].astype(BF16), preferred_element_type=F32)
        a = (_silu(gt) * up).astype(BF16)
        acc_ref[:n, :] += jnp.dot(a, wd_ref[...].astype(BF16), preferred_element_type=F32)

    if nsplit == 1:
        pl.when(active)(lambda: swiglu(acc_ref.shape[0]))
    else:
        pl.when(active & (nu_ref[1 + m] > rows))(lambda: swiglu(acc_ref.shape[0]))
        pl.when(active & (nu_ref[1 + m] <= rows))(lambda: swiglu(rows))

    @pl.when(f == nf - 1)
    def _():
        if residual:
            xn = x_ref[...] + gate_ref[...] * acc_ref[...]
            xo_ref[...] = xn
            h_ref[...] = _norm_mod(xn, g_ref[...], sc_ref[...], sh_ref[...]).astype(h_ref.dtype)
        else:
            y_ref[...] = _pack_halves(acc_ref[...])


def _pack_halves(x):
    bits = lax.bitcast_convert_type(x.astype(BF16).astype(F32), jnp.uint32)
    half = x.shape[1] // 2
    return (bits[:, :half] >> 16) | (bits[:, half:] & jnp.uint32(0xFFFF0000))


def _unpack_halves(w):
    lo = lax.bitcast_convert_type(w << 16, F32)
    hi = lax.bitcast_convert_type(w & jnp.uint32(0xFFFF0000), F32)
    return jnp.concatenate([lo, hi], axis=1)


def _ffn(xs, wg, wu, wd, li, tile_expert, tile_use, tm, tf, nsplit=1, epilogue=None):
    P, D = xs.shape
    F = wg.shape[-1]
    nf = F // tf
    nm = P // tm
    residual = epilogue is not None

    def fidx(m, f, te, nu):
        return jnp.where(m < nu[0], f, nf - 1)

    in_specs = [
        pl.BlockSpec((tm, D), lambda m, f, te, nu: (jnp.minimum(m, nu[0] - 1), 0)),
        pl.BlockSpec((None, None, D, tf), lambda m, f, te, nu: (li, te[m], 0, fidx(m, f, te, nu))),
        pl.BlockSpec((None, None, D, tf), lambda m, f, te, nu: (li, te[m], 0, fidx(m, f, te, nu))),
        pl.BlockSpec((None, None, tf, D), lambda m, f, te, nu: (li, te[m], fidx(m, f, te, nu), 0)),
    ]
    args = [xs, wg, wu, wd]
    row = pl.BlockSpec((tm, D), lambda m, f, te, nu: (m, 0))
    if residual:
        x2, gate, g, scale, shift, S, h_dtype = epilogue
        bvec = pl.BlockSpec((None, 1, D), lambda m, f, te, nu: ((m * tm) // S, 0, 0))
        in_specs += [row, bvec, pl.BlockSpec((1, D), lambda m, f, te, nu: (0, 0)), bvec, bvec]
        args += [x2, gate, g, scale, shift]
        out_specs = [row, row]
        out_shape = [jax.ShapeDtypeStruct((P, D), F32), jax.ShapeDtypeStruct((P, D), h_dtype)]
    else:
        out_specs = pl.BlockSpec((tm, D // 2), lambda m, f, te, nu: (m, 0))
        out_shape = jax.ShapeDtypeStruct((P, D // 2), jnp.uint32)
    return pl.pallas_call(
        functools.partial(_ffn_kernel, residual=residual, nf=nf, nsplit=nsplit),
        grid_spec=pltpu.PrefetchScalarGridSpec(
            num_scalar_prefetch=2,
            grid=(nm, nf),
            in_specs=in_specs,
            out_specs=out_specs,
            scratch_shapes=[pltpu.VMEM((tm, D), F32)] + ([] if residual else [pltpu.VMEM((tm, D), BF16)]),
        ),
        out_shape=out_shape,
        compiler_params=_params(("arbitrary", "arbitrary"), 56),
        name="ffn_res" if residual else "ffn_grouped",
    )(tile_expert, tile_use, *args)


def _router_kernel(h_ref, w_ref, b_ref, mi_ref, mw_ref, cnt_ref, carry_ref, *, E):
    i = pl.program_id(0)
    tm = h_ref.shape[0]

    @pl.when(i == 0)
    def _():
        carry_ref[...] = jnp.zeros_like(carry_ref)

    hf = h_ref[...]
    h_hi = hf.astype(BF16)
    h_lo = (hf - h_hi.astype(F32)).astype(BF16)
    w = w_ref[...]
    l2 = lax.dot_general(w, h_hi, _NT, preferred_element_type=F32)
    l1 = lax.dot_general(w[:E], h_lo, _NT, preferred_element_type=F32)
    logits = (l2[:E] + l2[E:] + l1) + b_ref[...]
    eid = lax.broadcasted_iota(jnp.int32, (E, tm), 0)
    m1 = jnp.max(logits, axis=0, keepdims=True)
    i1 = jnp.min(jnp.where(logits == m1, eid, E), axis=0, keepdims=True)
    rest = jnp.where(eid == i1, -jnp.inf, logits)
    m2 = jnp.max(rest, axis=0, keepdims=True)
    i2 = jnp.min(jnp.where(rest == m2, eid, E), axis=0, keepdims=True)
    e2 = jnp.exp(m2 - m1)
    inv = 1.0 / (1.0 + e2)
    oh1 = eid == i1
    oh2 = eid == i2
    cnt = jnp.where(oh1 | oh2, 1.0, 0.0)
    r_i = lax.broadcasted_iota(jnp.int32, (tm, tm), 0)
    c_i = lax.broadcasted_iota(jnp.int32, (tm, tm), 1)
    tri = jnp.where(r_i <= c_i, 1.0, 0.0).astype(BF16)
    incl = jnp.dot(cnt.astype(BF16), tri, preferred_element_type=F32)
    rank = carry_ref[...] + incl - cnt
    r1 = jnp.sum(jnp.where(oh1, rank, 0.0), axis=0, keepdims=True)
    r2 = jnp.sum(jnp.where(oh2, rank, 0.0), axis=0, keepdims=True)
    carry = carry_ref[...] + jnp.sum(cnt, axis=1, keepdims=True)
    carry_ref[...] = carry
    zi = jnp.zeros((4, tm), jnp.int32)
    mi_ref[...] = jnp.concatenate([i1, i2, r1.astype(jnp.int32), r2.astype(jnp.int32), zi], axis=0)
    mw_ref[...] = jnp.concatenate([inv, e2 * inv, jnp.zeros((6, tm), F32)], axis=0)
    cnt_ref[...] = jnp.broadcast_to(carry.astype(jnp.int32), cnt_ref.shape)


def _router(h, w_hl, b):
    N, D = h.shape
    E = b.shape[0]
    tm = _pick(N, 512)
    return pl.pallas_call(
        functools.partial(_router_kernel, E=E),
        grid=(N // tm,),
        in_specs=[pl.BlockSpec((tm, D), lambda i: (i, 0)), pl.BlockSpec((2 * E, D), lambda i: (0, 0)),
                  pl.BlockSpec((E, 1), lambda i: (0, 0))],
        out_specs=[pl.BlockSpec((8, tm), lambda i: (0, i)), pl.BlockSpec((8, tm), lambda i: (0, i)),
                   pl.BlockSpec((E, 128), lambda i: (0, 0))],
        out_shape=[jax.ShapeDtypeStruct((8, N), jnp.int32), jax.ShapeDtypeStruct((8, N), F32),
                   jax.ShapeDtypeStruct((E, 128), jnp.int32)],
        scratch_shapes=[pltpu.VMEM((E, 1), F32)],
        compiler_params=_params(("arbitrary",), 32),
        name="router",
    )(h, w_hl, b.reshape(E, 1))


def _pos_kernel(mi_ref, cnt_ref, pos_ref, te_ref, *, E, T):
    shift = T.bit_length() - 1
    cnt = cnt_ref[:, 0:1]
    padded = ((cnt + (T - 1)) >> shift) << shift
    eid = lax.broadcasted_iota(jnp.int32, (E, 1), 0)
    off = jnp.zeros((E, 1), jnp.int32)
    for j in range(E):
        off = off + jnp.where(eid > j, padded[j:j + 1, :], 0)
    end = off + padded
    e1, e2 = mi_ref[0:1, :], mi_ref[1:2, :]
    o1 = jnp.zeros_like(e1)
    o2 = jnp.zeros_like(e2)
    for j in range(E):
        o1 = o1 + jnp.where(e1 == j, off[j:j + 1, :], 0)
        o2 = o2 + jnp.where(e2 == j, off[j:j + 1, :], 0)
    zeros = jnp.zeros((6, pos_ref.shape[1]), jnp.int32)
    pos_ref[...] = jnp.concatenate([o1 + mi_ref[2:3, :], o2 + mi_ref[3:4, :], zeros], axis=0)
    tstart = lax.broadcasted_iota(jnp.int32, (1, te_ref.shape[1]), 1) * T
    te = jnp.zeros_like(tstart)
    for j in range(E):
        te = te + jnp.where(end[j:j + 1, :] <= tstart, 1, 0)
    te = jnp.minimum(te, E - 1)
    n_used = jnp.broadcast_to(end[E - 1:E, :] >> shift, te.shape)
    lane = lax.broadcasted_iota(jnp.int32, te.shape, 1)
    pad_lo = jnp.zeros_like(te)
    seg_end = jnp.zeros_like(te)
    tok_end = jnp.zeros_like(te)
    for j in range(E):
        pad_lo = pad_lo + jnp.where(lane == j, off[j:j + 1, :] + cnt[j:j + 1, :], 0)
        seg_end = seg_end + jnp.where(lane == j, end[j:j + 1, :], 0)
        tok_end = tok_end + jnp.where(te == j, off[j:j + 1, :] + cnt[j:j + 1, :], 0)
    tile_rows = jnp.clip(tok_end - tstart, 0, T)
    te_ref[...] = jnp.concatenate([te, n_used, pad_lo, seg_end, tile_rows, jnp.zeros((3, te.shape[1]), jnp.int32)],
                                  axis=0)


def _positions(meta_i, counts, T, n_tiles):
    N = meta_i.shape[1]
    E = counts.shape[0]
    ntp = ((n_tiles + 127) // 128) * 128
    return pl.pallas_call(
        functools.partial(_pos_kernel, E=E, T=T),
        out_shape=[jax.ShapeDtypeStruct((8, N), jnp.int32), jax.ShapeDtypeStruct((8, ntp), jnp.int32)],
        compiler_params=pltpu.CompilerParams(vmem_limit_bytes=32 * MIB),
        name="moe_positions",
    )(meta_i, counts)


_DMA_UNROLL = 8


def _dispatch_kernel(pos_ref, tab_ref, h_ref, xs_ref, stage_ref, zero_ref, sem, zsem, *, E):
    i = pl.program_id(0)
    n = pl.num_programs(0)
    tm = h_ref.shape[0]
    slot = lax.rem(i, 2)

    def row_copy(s, r, k, dst_row):
        return pltpu.make_async_copy(stage_ref.at[s, pl.ds(r, 1), :], xs_ref.at[pl.ds(dst_row, 1), :], sem.at[s])

    def wait_slot(s):
        def body(r, c):
            for k in range(TOP_K):
                row_copy(s, r, k, 0).wait()
            return c
        lax.fori_loop(0, tm, body, 0, unroll=_DMA_UNROLL)

    @pl.when(i >= 2)
    def _():
        wait_slot(slot)

    stage_ref[slot] = h_ref[...]

    def start(r, c):
        for k in range(TOP_K):
            row_copy(slot, r, k, pos_ref[k, r]).start(priority=k)
        return c

    lax.fori_loop(0, tm, start, 0, unroll=_DMA_UNROLL)

    @pl.when(i == n - 1)
    def _():
        zero_ref[...] = jnp.zeros_like(zero_ref)

        def pad_rows(issue):
            for e in range(E):
                def body(r, c):
                    cp = pltpu.make_async_copy(zero_ref.at[pl.ds(0, 1), :], xs_ref.at[pl.ds(r, 1), :], zsem)
                    if issue:
                        cp.start()
                    else:
                        cp.wait()
                    return c
                lax.fori_loop(tab_ref[2, e], tab_ref[3, e], body, 0)

        pad_rows(True)

        @pl.when(n >= 2)
        def _():
            wait_slot(1 - slot)
        wait_slot(slot)
        pad_rows(False)

        stage_ref[0] = jnp.zeros((tm, stage_ref.shape[2]), stage_ref.dtype)
        seg_end = tab_ref[3, E - 1]
        n_tail = (xs_ref.shape[0] - seg_end) // tm

        def tail_tiles(issue):
            def body(j, c):
                row0 = pl.multiple_of(seg_end + j * tm, tm)
                cp = pltpu.make_async_copy(stage_ref.at[0], xs_ref.at[pl.ds(row0, tm), :], zsem)
                if issue:
                    cp.start()
                else:
                    cp.wait()
                return c
            lax.fori_loop(0, n_tail, body, 0)

        tail_tiles(True)
        tail_tiles(False)


def _dispatch(pos, tile_tab, h, T, n_tiles, E):
    N, D = h.shape
    tm = _pick(T, 256)
    assert N % tm == 0
    P = n_tiles * T
    return pl.pallas_call(
        functools.partial(_dispatch_kernel, E=E),
        grid=(N // tm,),
        in_specs=[
            pl.BlockSpec((8, tm), lambda i: (0, i), memory_space=pltpu.SMEM),
            pl.BlockSpec(memory_space=pltpu.SMEM),
            pl.BlockSpec((tm, D), lambda i: (i, 0)),
        ],
        out_specs=pl.BlockSpec(memory_space=pl.ANY),
        out_shape=jax.ShapeDtypeStruct((P, D), h.dtype),
        scratch_shapes=[pltpu.VMEM((2, tm, D), h.dtype), pltpu.VMEM((8, D), h.dtype),
                        pltpu.SemaphoreType.DMA((2,)), pltpu.SemaphoreType.DMA(())],
        compiler_params=_params(("arbitrary",), 32),
        name="moe_dispatch",
    )(pos, tile_tab, h)


def _combine_kernel(pos_ref, posn_ref, w_ref, y_ref, x_ref, gate_ref, g_ref, sc_ref, sh_ref, xo_ref, h_ref,
                    buf_ref, sem):
    i = pl.program_id(0)
    n = pl.num_programs(0)
    tm = x_ref.shape[0]
    slot = lax.rem(i, 2)

    def row_copy(s, r, k, src_row):
        return pltpu.make_async_copy(y_ref.at[pl.ds(src_row, 1), :], buf_ref.at[s, k, pl.ds(r, 1), :], sem.at[s])

    def gather(p_ref, s):
        def body(r, c):
            for k in range(TOP_K):
                row_copy(s, r, k, p_ref[k, r]).start(priority=k)
            return c
        lax.fori_loop(0, tm, body, 0, unroll=_DMA_UNROLL)

    @pl.when(i == 0)
    def _():
        gather(pos_ref, 0)

    @pl.when(i + 1 < n)
    def _():
        gather(posn_ref, 1 - slot)

    def wait(r, c):
        for k in range(TOP_K):
            row_copy(slot, r, k, 0).wait()
        return c

    lax.fori_loop(0, tm, wait, 0, unroll=_DMA_UNROLL)
    w = w_ref[...]
    moe = w[:, 0:1] * _unpack_halves(buf_ref[slot, 0]) + w[:, 1:2] * _unpack_halves(buf_ref[slot, 1])
    xn = x_ref[...] + gate_ref[...] * moe
    xo_ref[...] = xn
    h_ref[...] = _norm_mod(xn, g_ref[...], sc_ref[...], sh_ref[...]).astype(h_ref.dtype)


def _combine(pos, w_col, y, x2, gate, g, scale, shift, S, h_dtype):
    N, D = x2.shape
    tm = _pick(S, 256)
    row = pl.BlockSpec((tm, D), lambda i: (i, 0))
    last = N // tm - 1
    return pl.pallas_call(
        _combine_kernel,
        grid=(N // tm,),
        in_specs=[
            pl.BlockSpec((8, tm), lambda i: (0, i), memory_space=pltpu.SMEM),
            pl.BlockSpec((8, tm), lambda i: (0, jnp.minimum(i + 1, last)), memory_space=pltpu.SMEM),
            pl.BlockSpec((tm, TOP_K), lambda i: (i, 0)),
            pl.BlockSpec(memory_space=pl.ANY),
            row,
            _row_vec_spec(D, tm, S),
            pl.BlockSpec((1, D), lambda i: (0, 0)),
            _row_vec_spec(D, tm, S),
            _row_vec_spec(D, tm, S),
        ],
        out_specs=[row, row],
        out_shape=[jax.ShapeDtypeStruct((N, D), F32), jax.ShapeDtypeStruct((N, D), h_dtype)],
        scratch_shapes=[pltpu.VMEM((2, TOP_K, tm, D // 2), jnp.uint32), pltpu.SemaphoreType.DMA((2,))],
        compiler_params=_params(("arbitrary",), 48),
        name="moe_combine",
    )(pos, pos, w_col, y, x2, gate, g, scale, shift)


def _split_hi_lo(w):
    hi = w.astype(BF16)
    lo = (w - hi.astype(F32)).astype(BF16)
    return hi, lo


def kernel(x, c, rel_bias, ada_w, ada_b, norm_g, a_wqkv, a_q_norm, a_k_norm, a_sinks, a_wo, b_wdown, b_q_a_norm, b_kv_a_norm, b_wuq, b_wukv, b_q_norm, b_k_norm, b_wo, ffn_wg, ffn_wu, ffn_wd, moe_router, moe_router_b, moe_wg, moe_wu, moe_wd):
    B, S, D = x.shape
    N = B * S
    depth = ada_w.shape[0]
    n_mod = ada_w.shape[2] // D
    Dh = a_q_norm.shape[-1]
    H = a_sinks.shape[-1]
    KV = (a_wqkv.shape[-1] // Dh - H) // 2
    G = H // KV
    QL = b_q_a_norm.shape[-1]
    KVL = b_kv_a_norm.shape[-1]
    R = b_wdown.shape[-1] - QL - KVL
    NOPE = b_q_norm.shape[-1] - R
    HB = b_wuq.shape[-1] // (NOPE + R)
    V = b_wukv.shape[-1] // HB - NOPE
    E = moe_router.shape[-1]

    rows = ((B + 15) // 16) * 16
    c_pad = jnp.zeros((rows, D), F32).at[:B].set(c)
    mod = _adaln(c_pad, ada_w, ada_b)[:, :B].reshape(depth, B, n_mod, D)
    mod_vec = lambda layer, j: mod[layer, :, j].reshape(B, 1, D)
    gain = lambda layer, j: norm_g[layer, j].reshape(1, D)

    bias_tab = _swa_bias_tables(rel_bias, KV, G)
    pos_f = jnp.arange(S, dtype=F32)
    inv = ROPE_THETA ** (-jnp.arange(0, R, 2, dtype=F32) / R)
    ang = pos_f[:, None] * inv[None, :]
    cos, sin = jnp.cos(ang), jnp.sin(ang)
    cos2 = jnp.concatenate([cos, cos], axis=-1)
    sin2 = jnp.concatenate([-sin, sin], axis=-1)
    swap = lambda t: jnp.concatenate([t[..., R // 2:], t[..., :R // 2]], axis=-1)

    x2 = x.reshape(N, D)
    h = None
    one_tile = jnp.zeros((N,), jnp.int32)
    ffn_w = [w.astype(BF16)[:, None] for w in (ffn_wg, ffn_wu, ffn_wd)]
    moe_w = (moe_wg, moe_wu, moe_wd)
    F_moe = moe_wg.shape[-1]

    for layer in range(depth):
        i = layer // 2
        moe_layer = layer % 2 == 1
        h2_dtype = F32 if moe_layer else BF16
        if not moe_layer:
            if layer == 0:
                qkv = _matmul(x2, a_wqkv[i].astype(BF16), S, norm=(gain(0, 0), mod_vec(0, 1), mod_vec(0, 0)))
            else:
                qkv = _matmul(h, a_wqkv[i].astype(BF16), S)
            o = _swa(qkv, bias_tab, a_q_norm[i], a_k_norm[i], a_sinks[i], B, S, H, KV, Dh)
            wo = a_wo[i]
        else:
            wd_ext = jnp.concatenate([b_wdown[i], swap(b_wdown[i][:, QL + KVL:])], axis=1).astype(BF16)
            kg = b_k_norm[i]
            cq, ckv, kpe = _mla_down(h, wd_ext, b_q_a_norm[i], b_kv_a_norm[i], kg[NOPE:], swap(kg[NOPE:]),
                                     cos2, sin2, S, QL, KVL, R)
            wq = b_wuq[i].reshape(QL, HB, NOPE + R)
            wq_h = jnp.concatenate([wq, swap(wq[..., NOPE:])], axis=-1).transpose(1, 0, 2).astype(BF16)
            wkv_h = b_wukv[i].reshape(KVL, HB, NOPE + V).transpose(1, 0, 2).astype(BF16)
            qg = b_q_norm[i]
            qf, kf, vf = _mla_up(cq, ckv, kpe, cos2, sin2, wq_h, wkv_h, qg[:NOPE], qg[NOPE:], swap(qg[NOPE:]),
                                 kg[:NOPE], B, S, NOPE, R, V, _pick(S, 512))
            o = _flash(qf, kf, vf)
            wo = b_wo[i]
        x2, h2 = _proj_res(o, wo.astype(BF16), x2, mod_vec(layer, 2), gain(layer, 1), mod_vec(layer, 4),
                           mod_vec(layer, 3), S, h2_dtype)

        nxt = min(layer + 1, depth - 1)
        nxt_args = (gain(nxt, 0), mod_vec(nxt, 1), mod_vec(nxt, 0), S, BF16)
        if not moe_layer:
            tm = _pick(S, 512)
            all_tiles = jnp.concatenate([jnp.full((1,), N // tm, jnp.int32), jnp.full((N // tm,), tm, jnp.int32)])
            x2, h = _ffn(h2, *ffn_w, i, one_tile[:N // tm], all_tiles, tm,
                         _pick(ffn_wg.shape[-1], 512), epilogue=(x2, mod_vec(layer, 5)) + nxt_args)
        else:
            T = _pick(N // E, 1024)
            n_tiles = (N * TOP_K) // T + E
            w_hi, w_lo = _split_hi_lo(moe_router[i].T)
            meta_i, meta_w, counts = _router(h2, jnp.concatenate([w_hi, w_lo], axis=0), moe_router_b[i])
            pos, tile_tab = _positions(meta_i, counts, T, n_tiles)
            xs = _dispatch(pos, tile_tab, h2, T, n_tiles, E)
            tile_use = jnp.concatenate([tile_tab[1, :1], tile_tab[4, :n_tiles]])
            y = _ffn(xs, *moe_w, i, tile_tab[0, :n_tiles], tile_use, T, _pick(F_moe, 256),
                     nsplit=2 if T >= 256 else 1)
            x2, h = _combine(pos, meta_w[:TOP_K].T, y, x2, mod_vec(layer, 5), *nxt_args)
    return x2.reshape(B, S, D)
```

```python
import functools
import math

import jax
import jax.numpy as jnp
from jax import lax
from jax.experimental import pallas as pl
from jax.experimental.pallas import tpu as pltpu

EPS = 1e-6
NEG_INF = -1e30
WINDOW = 128
BLOCK = 128
REL_MAX_DIST = 128
ROPE_THETA = 10000.0
TOP_K = 2
LOG2E = math.log2(math.e)

F32 = jnp.float32
BF16 = jnp.bfloat16
MIB = 1024 * 1024

_NT = (((1,), (1,)), ((), ()))


def _params(sem, vmem_mib, flags=None):
    return pltpu.CompilerParams(dimension_semantics=sem, vmem_limit_bytes=vmem_mib * MIB, flags=flags)


def _pick(n, pref):
    t = min(pref, n)
    while n % t:
        t //= 2
    return t


def _rms(x):
    return x * lax.rsqrt(jnp.mean(x * x, axis=-1, keepdims=True) + EPS)


def _norm_mod(x, g, scale, shift):
    return (_rms(x) * g) * (1.0 + scale) + shift


def _silu(x):
    return x * (1.0 / (1.0 + jnp.exp(-x)))


def _resident(shape, index_map):
    return pl.BlockSpec(shape, index_map, pipeline_mode=pl.Buffered(1))


def _adaln_kernel(c_ref, w_ref, b_ref, o_ref):
    ca = _silu(c_ref[...]).astype(BF16)
    o_ref[...] = jnp.dot(ca, w_ref[...].astype(BF16), preferred_element_type=F32) + b_ref[...]


def _adaln(c_pad, ada_w, ada_b):
    L, D, N6 = ada_w.shape
    R = c_pad.shape[0]
    tn = _pick(N6, 1536)
    return pl.pallas_call(
        _adaln_kernel,
        grid=(L, N6 // tn),
        in_specs=[
            pl.BlockSpec((R, D), lambda l, j: (0, 0)),
            pl.BlockSpec((None, D, tn), lambda l, j: (l, 0, j)),
            pl.BlockSpec((None, 1, tn), lambda l, j: (l, 0, j)),
        ],
        out_specs=pl.BlockSpec((None, R, tn), lambda l, j: (l, 0, j)),
        out_shape=jax.ShapeDtypeStruct((L, R, N6), F32),
        compiler_params=_params(("arbitrary", "arbitrary"), 48),
        name="adaln",
    )(c_pad, ada_w, ada_b.reshape(L, 1, N6))


def _row_vec_spec(D, tm, S):
    return pl.BlockSpec((None, 1, D), lambda i: ((i * tm) // S, 0, 0))


def _matmul_kernel(a_ref, w_ref, *refs, tn, norm):
    if norm:
        g_ref, sc_ref, sh_ref, o_ref = refs
        a = _norm_mod(a_ref[...], g_ref[...], sc_ref[...], sh_ref[...]).astype(BF16)
    else:
        (o_ref,) = refs
        a = a_ref[...]
    for j in range(0, o_ref.shape[-1], tn):
        o_ref[:, j:j + tn] = jnp.dot(a, w_ref[:, j:j + tn], preferred_element_type=F32).astype(o_ref.dtype)


def _matmul(a, w, S, norm=None, out_dtype=BF16):
    N, K = a.shape
    Nout = w.shape[1]
    tm = _pick(S, 512)
    tn = _pick(Nout, 512)
    in_specs = [pl.BlockSpec((tm, K), lambda i: (i, 0)), _resident((K, Nout), lambda i: (0, 0))]
    args = [a, w]
    if norm is not None:
        in_specs += [pl.BlockSpec((1, K), lambda i: (0, 0)), _row_vec_spec(K, tm, S), _row_vec_spec(K, tm, S)]
        args += list(norm)
    return pl.pallas_call(
        functools.partial(_matmul_kernel, tn=tn, norm=norm is not None),
        grid=(N // tm,),
        in_specs=in_specs,
        out_specs=pl.BlockSpec((tm, Nout), lambda i: (i, 0)),
        out_shape=jax.ShapeDtypeStruct((N, Nout), out_dtype),
        compiler_params=_params(("arbitrary",), 48),
        name="matmul",
    )(*args)


def _t5_bucket(n, n_buckets):
    max_exact = n_buckets // 2
    nf = jnp.maximum(n, 1).astype(F32)
    large = max_exact + (jnp.log(nf / max_exact) / math.log(REL_MAX_DIST / max_exact)
                         * (n_buckets - max_exact)).astype(jnp.int32)
    large = jnp.minimum(large, n_buckets - 1)
    return jnp.where(n < max_exact, n, large)


def _swa_bias_tables(rel_bias, KV, G):
    t_idx = jnp.arange(BLOCK)[:, None]
    j_idx = jnp.arange(2 * BLOCK)[None, :]
    dist = t_idx + BLOCK - j_idx
    in_window = (dist >= 0) & (dist < WINDOW)
    n_buckets = rel_bias.shape[0]
    onehot = jax.nn.one_hot(_t5_bucket(jnp.maximum(dist, 0), n_buckets).reshape(-1), n_buckets, dtype=F32)
    bias = jnp.dot(onehot, rel_bias.astype(F32), precision=lax.Precision.HIGHEST)
    bias = bias.reshape(BLOCK, 2 * BLOCK, -1).transpose(2, 0, 1) * LOG2E
    general = jnp.where(in_window[None], bias, NEG_INF)
    first = jnp.where((in_window & (j_idx >= BLOCK))[None], bias, NEG_INF)
    tabs = jnp.stack([first, general]).reshape(2, KV, G, BLOCK, 2 * BLOCK)
    return tabs.transpose(0, 1, 4, 2, 3).reshape(2, KV, 2 * BLOCK, G * BLOCK)


def _swa_kernel(q_ref, kp_ref, kc_ref, vp_ref, vc_ref, bias_ref, qg_ref, kg_ref, sink_ref, o_ref, *, KV, G, Dh):
    T = BLOCK
    qt = q_ref[...].astype(F32).T
    k_all = jnp.concatenate([kp_ref[...], kc_ref[...]], axis=0).astype(F32)
    vt = jnp.concatenate([vp_ref[...], vc_ref[...]], axis=0).astype(F32).T
    qts = []
    for h in range(KV * G):
        x = qt[h * Dh:(h + 1) * Dh, :]
        r = lax.rsqrt(jnp.mean(x * x, axis=0, keepdims=True) + EPS)
        qts.append(((x * r) * qg_ref[...]).astype(BF16))
    scores = []
    for kv in range(KV):
        kh = (_rms(k_all[:, kv * Dh:(kv + 1) * Dh]) * kg_ref[...]).astype(BF16)
        qs_t = jnp.concatenate(qts[kv * G:(kv + 1) * G], axis=1)
        scores.append(jnp.dot(kh, qs_t, preferred_element_type=F32))
    probs = []
    for kv in range(KV):
        s = scores[kv] + bias_ref[kv]
        sink = sink_ref[kv]
        m = jnp.maximum(jnp.max(s, axis=0, keepdims=True), sink)
        p = jnp.exp2(s - m)
        denom = jnp.sum(p, axis=0, keepdims=True) + jnp.exp2(sink - m)
        probs.append((p.astype(BF16), 1.0 / denom))
    outs_t = []
    for kv in range(KV):
        p, inv = probs[kv]
        o_t = jnp.dot(vt[kv * Dh:(kv + 1) * Dh, :].astype(BF16), p, preferred_element_type=F32) * inv
        outs_t += [o_t[:, g * T:(g + 1) * T] for g in range(G)]
    o_ref[...] = jnp.concatenate(outs_t, axis=0).T.astype(o_ref.dtype)


def _swa(qkv, bias_tab, q_g, k_g, sinks, B, S, H, KV, Dh):
    G = H // KV
    nb = S // BLOCK
    HD, KD = H * Dh, KV * Dh
    assert HD % KD == 0
    kcol = HD // KD
    qkv3 = qkv.reshape(B, S, HD + 2 * KD)
    sink_row = jnp.broadcast_to((sinks.astype(F32) * LOG2E).reshape(KV, 1, G, 1),
                                (KV, 1, G, BLOCK)).reshape(KV, 1, G * BLOCK)
    q_gt = jnp.broadcast_to((q_g * (Dh ** -0.5 * LOG2E))[:, None], (Dh, BLOCK))
    prev = lambda b, i: jnp.maximum(i - 1, 0)
    const = lambda shape: pl.BlockSpec(shape, lambda b, i: (0,) * len(shape))
    out = pl.pallas_call(
        functools.partial(_swa_kernel, KV=KV, G=G, Dh=Dh),
        grid=(B, nb),
        in_specs=[
            pl.BlockSpec((None, BLOCK, HD), lambda b, i: (b, i, 0)),
            pl.BlockSpec((None, BLOCK, KD), lambda b, i: (b, prev(b, i), kcol)),
            pl.BlockSpec((None, BLOCK, KD), lambda b, i: (b, i, kcol)),
            pl.BlockSpec((None, BLOCK, KD), lambda b, i: (b, prev(b, i), kcol + 1)),
            pl.BlockSpec((None, BLOCK, KD), lambda b, i: (b, i, kcol + 1)),
            pl.BlockSpec((None, KV, 2 * BLOCK, G * BLOCK), lambda b, i: (jnp.minimum(i, 1), 0, 0, 0)),
            const((Dh, BLOCK)), const((1, Dh)), const((KV, 1, G * BLOCK)),
        ],
        out_specs=pl.BlockSpec((None, BLOCK, HD), lambda b, i: (b, i, 0)),
        out_shape=jax.ShapeDtypeStruct((B, S, HD), BF16),
        compiler_params=_params(("arbitrary", "arbitrary"), 48),
        name="swa",
    )(qkv3, qkv3, qkv3, qkv3, qkv3, bias_tab, q_gt, k_g.reshape(1, Dh), sink_row)
    return out.reshape(B * S, HD)


def _down_kernel(a_ref, w_ref, qag_ref, kvag_ref, kg_ref, kgs_ref, cos_ref, sin_ref,
                 cq_ref, ckv_ref, kpe_ref, *, QL, KVL, R):
    lat = jnp.dot(a_ref[...], w_ref[...], preferred_element_type=F32)
    cq_ref[...] = (_rms(lat[:, :QL]) * qag_ref[...]).astype(cq_ref.dtype)
    ckv_ref[...] = (_rms(lat[:, QL:QL + KVL]) * kvag_ref[...]).astype(ckv_ref.dtype)
    kpe = lat[:, QL + KVL:QL + KVL + R]
    kpe_sw = lat[:, QL + KVL + R:]
    r = lax.rsqrt(jnp.mean(kpe * kpe, axis=-1, keepdims=True) + EPS)
    kpe_ref[...] = (kpe * r * kg_ref[...]) * cos_ref[...] + (kpe_sw * r * kgs_ref[...]) * sin_ref[...]


def _mla_down(h, w_ext, qa_g, kva_g, kg_pe, kg_pe_sw, cos2, sin2, S, QL, KVL, R):
    N, D = h.shape
    tm = _pick(S, 512)
    W = w_ext.shape[1]
    ns = S // tm
    vec = lambda n: pl.BlockSpec((1, n), lambda i: (0, 0))
    pos_spec = pl.BlockSpec((tm, R), lambda i: (i % ns, 0))
    return pl.pallas_call(
        functools.partial(_down_kernel, QL=QL, KVL=KVL, R=R),
        grid=(N // tm,),
        in_specs=[pl.BlockSpec((tm, D), lambda i: (i, 0)), _resident((D, W), lambda i: (0, 0)),
                  vec(QL), vec(KVL), vec(R), vec(R), pos_spec, pos_spec],
        out_specs=[pl.BlockSpec((tm, QL), lambda i: (i, 0)), pl.BlockSpec((tm, KVL), lambda i: (i, 0)),
                   pl.BlockSpec((tm, R), lambda i: (i, 0))],
        out_shape=[jax.ShapeDtypeStruct((N, QL), BF16), jax.ShapeDtypeStruct((N, KVL), BF16),
                   jax.ShapeDtypeStruct((N, R), F32)],
        compiler_params=_params(("arbitrary",), 40),
        name="mla_down",
    )(h, w_ext, qa_g.reshape(1, QL), kva_g.reshape(1, KVL), kg_pe.reshape(1, R), kg_pe_sw.reshape(1, R), cos2, sin2)


def _up_kernel(cq_ref, ckv_ref, kpe_ref, cos_ref, sin_ref, wq_ref, wkv_ref, qgn_ref, qgp_ref, qgps_ref, kgn_ref,
               q_ref, k_ref, v_ref, *, NOPE, R):
    HP = q_ref.shape[0]
    h0 = pl.program_id(1) * HP
    cq, ckv = cq_ref[...], ckv_ref[...]
    qhs = [jnp.dot(cq, wq_ref[h0 + j], preferred_element_type=F32) for j in range(HP)]
    kvhs = [jnp.dot(ckv, wkv_ref[h0 + j], preferred_element_type=F32) for j in range(HP)]
    kpe = kpe_ref[...].astype(k_ref.dtype)
    tk = v_ref.shape[-1]
    for j in range(HP):
        qh, kvh = qhs[j], kvhs[j]
        qn, qp, qps = qh[:, :NOPE], qh[:, NOPE:NOPE + R], qh[:, NOPE + R:]
        q_ref[j, :, :NOPE] = (_rms(qn) * qgn_ref[...]).astype(q_ref.dtype)
        rp = lax.rsqrt(jnp.mean(qp * qp, axis=-1, keepdims=True) + EPS)
        q_rot = (qp * rp * qgp_ref[...]) * cos_ref[...] + (qps * rp * qgps_ref[...]) * sin_ref[...]
        q_ref[j, :, NOPE:] = q_rot.astype(q_ref.dtype)
        k_ref[j, :, :NOPE] = (_rms(kvh[:, :NOPE]) * kgn_ref[...]).astype(k_ref.dtype)
        k_ref[j, :, NOPE:] = kpe
        for c in range(v_ref.shape[1]):
            v_ref[j, c] = kvh[c * tk:(c + 1) * tk, NOPE:].T.astype(v_ref.dtype)


def _mla_up(cq, ckv, kpe, cos2, sin2, wq_h, wkv_h, qg_n, qg_p, qg_ps, kg_n, B, S, NOPE, R, V, tk):
    N, QL = cq.shape
    KVL = ckv.shape[1]
    H = wq_h.shape[0]
    HP = _pick(H, 8)
    tm = _pick(S, 512)
    ns = S // tm
    nc = tm // tk
    scale = (NOPE + R) ** -0.5 * LOG2E
    row = lambda n: pl.BlockSpec((tm, n), lambda i, h: (i, 0))
    pos_spec = pl.BlockSpec((tm, R), lambda i, h: (i % ns, 0))
    vec = lambda n: pl.BlockSpec((1, n), lambda i, h: (0, 0))
    head_out = lambda n: pl.BlockSpec((None, HP, tm, n), lambda i, h: (i // ns, h, i % ns, 0))
    return pl.pallas_call(
        functools.partial(_up_kernel, NOPE=NOPE, R=R),
        grid=(N // tm, H // HP),
        in_specs=[row(QL), row(KVL), row(R), pos_spec, pos_spec,
                  _resident(wq_h.shape, lambda i, h: (0, 0, 0)), _resident(wkv_h.shape, lambda i, h: (0, 0, 0)),
                  vec(NOPE), vec(R), vec(R), vec(NOPE)],
        out_specs=[head_out(NOPE + R), head_out(NOPE + R),
                   pl.BlockSpec((None, HP, nc, V, tk), lambda i, h: (i // ns, h, i % ns, 0, 0))],
        out_shape=[jax.ShapeDtypeStruct((B, H, S, NOPE + R), BF16), jax.ShapeDtypeStruct((B, H, S, NOPE + R), BF16),
                   jax.ShapeDtypeStruct((B, H, S // tk, V, tk), BF16)],
        compiler_params=_params(("arbitrary", "arbitrary"), 48),
        name="mla_up",
    )(cq, ckv, kpe, cos2, sin2, wq_h, wkv_h, (qg_n * scale).reshape(1, NOPE), (qg_p * scale).reshape(1, R),
      (qg_ps * scale).reshape(1, R), kg_n.reshape(1, NOPE))


def _flash_kernel(q_ref, k_ref, vt_ref, o_ref, sa_ref, sb_ref, *, tq):
    qi = pl.program_id(2)
    HP, V, tk = vt_ref.shape[0], vt_ref.shape[2], vt_ref.shape[3]
    assert tk == tq

    def qk(ki, dst_ref):
        start = pl.multiple_of(ki * tk, tk)
        for hh in range(HP):
            dst_ref[hh] = lax.dot_general(k_ref[hh, pl.ds(start, tk), :], q_ref[hh], _NT,
                                          preferred_element_type=F32)

    def consume(src_ref, ki, carry, diagonal):
        probs = []
        for hh in range(HP):
            m, l, _ = carry[hh]
            s = src_ref[hh]
            if diagonal:
                kpos = lax.broadcasted_iota(jnp.int32, s.shape, 0)
                qpos = lax.broadcasted_iota(jnp.int32, s.shape, 1)
                s = jnp.where(kpos <= qpos, s, NEG_INF)
            m_new = jnp.maximum(m, jnp.max(s, axis=0, keepdims=True))
            alpha = jnp.exp2(m - m_new)
            p = jnp.exp2(s - m_new)
            probs.append((m_new, alpha, alpha * l + jnp.sum(p, axis=0, keepdims=True), p.astype(BF16)))
        out = []
        for hh in range(HP):
            m_new, alpha, l, p = probs[hh]
            acc = alpha * carry[hh][2] + jnp.dot(vt_ref[hh, ki], p, preferred_element_type=F32)
            out.append((m_new, l, acc))
        return tuple(out)

    def pair(kk, carry):
        k0 = 2 * kk
        qk(k0 + 1, sb_ref)
        carry = consume(sa_ref, k0, carry, False)
        qk(k0 + 2, sa_ref)
        return consume(sb_ref, k0 + 1, carry, False)

    def last_even(carry):
        return consume(sa_ref, qi, carry, True)

    def last_odd(carry):
        qk(qi, sb_ref)
        return consume(sb_ref, qi, consume(sa_ref, qi - 1, carry, False), True)

    init = (jnp.full((1, tq), NEG_INF, F32), jnp.zeros((1, tq), F32), jnp.zeros((V, tq), F32))
    qk(0, sa_ref)
    carry = lax.fori_loop(0, qi >> 1, pair, (init,) * HP)
    carry = lax.cond((qi & 1) == 0, last_even, last_odd, carry)
    for hh in range(HP):
        _, l, acc = carry[hh]
        o_ref[:, hh * V:(hh + 1) * V] = (acc * (1.0 / l)).T.astype(o_ref.dtype)


def _flash(q, k, vt, heads_per_step=4):
    B, H, S, DK = q.shape
    _, _, nk, V, tk = vt.shape
    tq = _pick(S, 512)
    HP = _pick(H, heads_per_step)
    out = pl.pallas_call(
        functools.partial(_flash_kernel, tq=tq),
        grid=(B, H // HP, S // tq),
        in_specs=[
            pl.BlockSpec((None, HP, tq, DK), lambda b, h, i: (b, h, i, 0)),
            pl.BlockSpec((None, HP, S, DK), lambda b, h, i: (b, h, 0, 0)),
            pl.BlockSpec((None, HP, nk, V, tk), lambda b, h, i: (b, h, 0, 0, 0)),
        ],
        out_specs=pl.BlockSpec((None, tq, HP * V), lambda b, h, i: (b, i, h)),
        out_shape=jax.ShapeDtypeStruct((B, S, H * V), BF16),
        scratch_shapes=[pltpu.VMEM((HP, tk, tq), F32), pltpu.VMEM((HP, tk, tq), F32)],
        compiler_params=_params(("arbitrary", "arbitrary", "arbitrary"), 56),
        name="mla_flash",
    )(q, k, vt)
    return out.reshape(B * S, H * V)


def _proj_res_kernel(a_ref, w_ref, x_ref, gate_ref, g_ref, sc_ref, sh_ref, xo_ref, h_ref):
    y = jnp.dot(a_ref[...], w_ref[...], preferred_element_type=F32)
    xn = x_ref[...] + gate_ref[...] * y
    xo_ref[...] = xn
    h_ref[...] = _norm_mod(xn, g_ref[...], sc_ref[...], sh_ref[...]).astype(h_ref.dtype)


def _proj_res(a, w, x2, gate, g, scale, shift, S, h_dtype):
    N, K = a.shape
    D = w.shape[1]
    tm = _pick(S, 512)
    return pl.pallas_call(
        _proj_res_kernel,
        grid=(N // tm,),
        in_specs=[
            pl.BlockSpec((tm, K), lambda i: (i, 0)),
            _resident((K, D), lambda i: (0, 0)),
            pl.BlockSpec((tm, D), lambda i: (i, 0)),
            _row_vec_spec(D, tm, S),
            pl.BlockSpec((1, D), lambda i: (0, 0)),
            _row_vec_spec(D, tm, S),
            _row_vec_spec(D, tm, S),
        ],
        out_specs=[pl.BlockSpec((tm, D), lambda i: (i, 0)), pl.BlockSpec((tm, D), lambda i: (i, 0))],
        out_shape=[jax.ShapeDtypeStruct((N, D), F32), jax.ShapeDtypeStruct((N, D), h_dtype)],
        compiler_params=_params(("arbitrary",), 48),
        name="proj_res",
    )(a, w, x2, gate, g, scale, shift)


def _ffn_kernel(te_ref, nu_ref, *refs, residual, nf, nsplit):
    if residual:
        xs_ref, wg_ref, wu_ref, wd_ref, x_ref, gate_ref, g_ref, sc_ref, sh_ref, xo_ref, h_ref, acc_ref = refs
        xb_ref = xs_ref
    else:
        xs_ref, wg_ref, wu_ref, wd_ref, y_ref, acc_ref, xb_ref = refs
    m = pl.program_id(0)
    f = pl.program_id(1)
    active = m < nu_ref[0]
    rows = acc_ref.shape[0] // nsplit

    @pl.when(f == 0)
    def _():
        acc_ref[...] = jnp.zeros_like(acc_ref)
        if not residual:
            xb_ref[...] = xs_ref[...].astype(BF16)

    def swiglu(n):
        xb = xb_ref[:n, :]
        gt = jnp.dot(xb, wg_ref[...].astype(BF16), preferred_element_type=F32)
        up = jnp.dot(xb, wu_ref[...].astype(BF16), preferred_element_type=F32)
        a = (_silu(gt) * up).astype(BF16)
        acc_ref[:n, :] += jnp.dot(a, wd_ref[...].astype(BF16), preferred_element_type=F32)

    if nsplit == 1:
        pl.when(active)(lambda: swiglu(acc_ref.shape[0]))
    else:
        pl.when(active & (nu_ref[1 + m] > rows))(lambda: swiglu(acc_ref.shape[0]))
        pl.when(active & (nu_ref[1 + m] <= rows))(lambda: swiglu(rows))

    @pl.when(f == nf - 1)
    def _():
        if residual:
            xn = x_ref[...] + gate_ref[...] * acc_ref[...]
            xo_ref[...] = xn
            h_ref[...] = _norm_mod(xn, g_ref[...], sc_ref[...], sh_ref[...]).astype(h_ref.dtype)
        else:
            y_ref[...] = _pack_halves(acc_ref[...])


def _pack_halves(x):
    bits = lax.bitcast_convert_type(x.astype(BF16).astype(F32), jnp.uint32)
    half = x.shape[1] // 2
    return (bits[:, :half] >> 16) | (bits[:, half:] & jnp.uint32(0xFFFF0000))


def _unpack_halves(w):
    lo = lax.bitcast_convert_type(w << 16, F32)
    hi = lax.bitcast_convert_type(w & jnp.uint32(0xFFFF0000), F32)
    return jnp.concatenate([lo, hi], axis=1)


def _ffn(xs, wg, wu, wd, li, tile_expert, tile_use, tm, tf, nsplit=1, epilogue=None):
    P, D = xs.shape
    F = wg.shape[-1]
    nf = F // tf
    nm = P // tm
    residual = epilogue is not None

    def fidx(m, f, te, nu):
        return jnp.where(m < nu[0], f, nf - 1)

    in_specs = [
        pl.BlockSpec((tm, D), lambda m, f, te, nu: (jnp.minimum(m, nu[0] - 1), 0)),
        pl.BlockSpec((None, None, D, tf), lambda m, f, te, nu: (li, te[m], 0, fidx(m, f, te, nu))),
        pl.BlockSpec((None, None, D, tf), lambda m, f, te, nu: (li, te[m], 0, fidx(m, f, te, nu))),
        pl.BlockSpec((None, None, tf, D), lambda m, f, te, nu: (li, te[m], fidx(m, f, te, nu), 0)),
    ]
    args = [xs, wg, wu, wd]
    row = pl.BlockSpec((tm, D), lambda m, f, te, nu: (m, 0))
    if residual:
        x2, gate, g, scale, shift, S, h_dtype = epilogue
        bvec = pl.BlockSpec((None, 1, D), lambda m, f, te, nu: ((m * tm) // S, 0, 0))
        in_specs += [row, bvec, pl.BlockSpec((1, D), lambda m, f, te, nu: (0, 0)), bvec, bvec]
        args += [x2, gate, g, scale, shift]
        out_specs = [row, row]
        out_shape = [jax.ShapeDtypeStruct((P, D), F32), jax.ShapeDtypeStruct((P, D), h_dtype)]
    else:
        out_specs = pl.BlockSpec((tm, D // 2), lambda m, f, te, nu: (m, 0))
        out_shape = jax.ShapeDtypeStruct((P, D // 2), jnp.uint32)
    return pl.pallas_call(
        functools.partial(_ffn_kernel, residual=residual, nf=nf, nsplit=nsplit),
        grid_spec=pltpu.PrefetchScalarGridSpec(
            num_scalar_prefetch=2,
            grid=(nm, nf),
            in_specs=in_specs,
            out_specs=out_specs,
            scratch_shapes=[pltpu.VMEM((tm, D), F32)] + ([] if residual else [pltpu.VMEM((tm, D), BF16)]),
        ),
        out_shape=out_shape,
        compiler_params=_params(("arbitrary", "arbitrary"), 56),
        name="ffn_res" if residual else "ffn_grouped",
    )(tile_expert, tile_use, *args)


def _router_kernel(h_ref, w_ref, b_ref, mi_ref, mw_ref, cnt_ref, carry_ref, *, E):
    i = pl.program_id(0)
    tm = h_ref.shape[0]

    @pl.when(i == 0)
    def _():
        carry_ref[...] = jnp.zeros_like(carry_ref)

    hf = h_ref[...]
    h_hi = hf.astype(BF16)
    h_lo = (hf - h_hi.astype(F32)).astype(BF16)
    w = w_ref[...]
    l2 = lax.dot_general(w, h_hi, _NT, preferred_element_type=F32)
    l1 = lax.dot_general(w[:E], h_lo, _NT, preferred_element_type=F32)
    logits = (l2[:E] + l2[E:] + l1) + b_ref[...]
    eid = lax.broadcasted_iota(jnp.int32, (E, tm), 0)
    m1 = jnp.max(logits, axis=0, keepdims=True)
    i1 = jnp.min(jnp.where(logits == m1, eid, E), axis=0, keepdims=True)
    rest = jnp.where(eid == i1, -jnp.inf, logits)
    m2 = jnp.max(rest, axis=0, keepdims=True)
    i2 = jnp.min(jnp.where(rest == m2, eid, E), axis=0, keepdims=True)
    e2 = jnp.exp(m2 - m1)
    inv = 1.0 / (1.0 + e2)
    oh1 = eid == i1
    oh2 = eid == i2
    cnt = jnp.where(oh1 | oh2, 1.0, 0.0)
    r_i = lax.broadcasted_iota(jnp.int32, (tm, tm), 0)
    c_i = lax.broadcasted_iota(jnp.int32, (tm, tm), 1)
    tri = jnp.where(r_i <= c_i, 1.0, 0.0).astype(BF16)
    incl = jnp.dot(cnt.astype(BF16), tri, preferred_element_type=F32)
    rank = carry_ref[...] + incl - cnt
    r1 = jnp.sum(jnp.where(oh1, rank, 0.0), axis=0, keepdims=True)
    r2 = jnp.sum(jnp.where(oh2, rank, 0.0), axis=0, keepdims=True)
    carry = carry_ref[...] + jnp.sum(cnt, axis=1, keepdims=True)
    carry_ref[...] = carry
    zi = jnp.zeros((4, tm), jnp.int32)
    mi_ref[...] = jnp.concatenate([i1, i2, r1.astype(jnp.int32), r2.astype(jnp.int32), zi], axis=0)
    mw_ref[...] = jnp.concatenate([inv, e2 * inv, jnp.zeros((6, tm), F32)], axis=0)
    cnt_ref[...] = jnp.broadcast_to(carry.astype(jnp.int32), cnt_ref.shape)


def _router(h, w_hl, b):
    N, D = h.shape
    E = b.shape[0]
    tm = _pick(N, 512)
    return pl.pallas_call(
        functools.partial(_router_kernel, E=E),
        grid=(N // tm,),
        in_specs=[pl.BlockSpec((tm, D), lambda i: (i, 0)), pl.BlockSpec((2 * E, D), lambda i: (0, 0)),
                  pl.BlockSpec((E, 1), lambda i: (0, 0))],
        out_specs=[pl.BlockSpec((8, tm), lambda i: (0, i)), pl.BlockSpec((8, tm), lambda i: (0, i)),
                   pl.BlockSpec((E, 128), lambda i: (0, 0))],
        out_shape=[jax.ShapeDtypeStruct((8, N), jnp.int32), jax.ShapeDtypeStruct((8, N), F32),
                   jax.ShapeDtypeStruct((E, 128), jnp.int32)],
        scratch_shapes=[pltpu.VMEM((E, 1), F32)],
        compiler_params=_params(("arbitrary",), 32),
        name="router",
    )(h, w_hl, b.reshape(E, 1))


def _pos_kernel(mi_ref, cnt_ref, pos_ref, te_ref, *, E, T):
    shift = T.bit_length() - 1
    cnt = cnt_ref[:, 0:1]
    padded = ((cnt + (T - 1)) >> shift) << shift
    eid = lax.broadcasted_iota(jnp.int32, (E, 1), 0)
    off = jnp.zeros((E, 1), jnp.int32)
    for j in range(E):
        off = off + jnp.where(eid > j, padded[j:j + 1, :], 0)
    end = off + padded
    e1, e2 = mi_ref[0:1, :], mi_ref[1:2, :]
    o1 = jnp.zeros_like(e1)
    o2 = jnp.zeros_like(e2)
    for j in range(E):
        o1 = o1 + jnp.where(e1 == j, off[j:j + 1, :], 0)
        o2 = o2 + jnp.where(e2 == j, off[j:j + 1, :], 0)
    zeros = jnp.zeros((6, pos_ref.shape[1]), jnp.int32)
    pos_ref[...] = jnp.concatenate([o1 + mi_ref[2:3, :], o2 + mi_ref[3:4, :], zeros], axis=0)
    tstart = lax.broadcasted_iota(jnp.int32, (1, te_ref.shape[1]), 1) * T
    te = jnp.zeros_like(tstart)
    for j in range(E):
        te = te + jnp.where(end[j:j + 1, :] <= tstart, 1, 0)
    te = jnp.minimum(te, E - 1)
    n_used = jnp.broadcast_to(end[E - 1:E, :] >> shift, te.shape)
    lane = lax.broadcasted_iota(jnp.int32, te.shape, 1)
    pad_lo = jnp.zeros_like(te)
    seg_end = jnp.zeros_like(te)
    tok_end = jnp.zeros_like(te)
    for j in range(E):
        pad_lo = pad_lo + jnp.where(lane == j, off[j:j + 1, :] + cnt[j:j + 1, :], 0)
        seg_end = seg_end + jnp.where(lane == j, end[j:j + 1, :], 0)
        tok_end = tok_end + jnp.where(te == j, off[j:j + 1, :] + cnt[j:j + 1, :], 0)
    tile_rows = jnp.clip(tok_end - tstart, 0, T)
    te_ref[...] = jnp.concatenate([te, n_used, pad_lo, seg_end, tile_rows, jnp.zeros((3, te.shape[1]), jnp.int32)],
                                  axis=0)


def _positions(meta_i, counts, T, n_tiles):
    N = meta_i.shape[1]
    E = counts.shape[0]
    ntp = ((n_tiles + 127) // 128) * 128
    return pl.pallas_call(
        functools.partial(_pos_kernel, E=E, T=T),
        out_shape=[jax.ShapeDtypeStruct((8, N), jnp.int32), jax.ShapeDtypeStruct((8, ntp), jnp.int32)],
        compiler_params=pltpu.CompilerParams(vmem_limit_bytes=32 * MIB),
        name="moe_positions",
    )(meta_i, counts)


_DMA_UNROLL = 8


def _dispatch_kernel(pos_ref, tab_ref, h_ref, xs_ref, stage_ref, zero_ref, sem, zsem, *, E):
    i = pl.program_id(0)
    n = pl.num_programs(0)
    tm = h_ref.shape[0]
    slot = lax.rem(i, 2)

    def row_copy(s, r, k, dst_row):
        return pltpu.make_async_copy(stage_ref.at[s, pl.ds(r, 1), :], xs_ref.at[pl.ds(dst_row, 1), :], sem.at[s])

    def wait_slot(s):
        def body(r, c):
            for k in range(TOP_K):
                row_copy(s, r, k, 0).wait()
            return c
        lax.fori_loop(0, tm, body, 0, unroll=_DMA_UNROLL)

    @pl.when(i >= 2)
    def _():
        wait_slot(slot)

    stage_ref[slot] = h_ref[...]

    def start(r, c):
        for k in range(TOP_K):
            row_copy(slot, r, k, pos_ref[k, r]).start(priority=k)
        return c

    lax.fori_loop(0, tm, start, 0, unroll=_DMA_UNROLL)

    @pl.when(i == n - 1)
    def _():
        zero_ref[...] = jnp.zeros_like(zero_ref)

        def pad_rows(issue):
            for e in range(E):
                def body(r, c):
                    cp = pltpu.make_async_copy(zero_ref.at[pl.ds(0, 1), :], xs_ref.at[pl.ds(r, 1), :], zsem)
                    if issue:
                        cp.start()
                    else:
                        cp.wait()
                    return c
                lax.fori_loop(tab_ref[2, e], tab_ref[3, e], body, 0)

        pad_rows(True)

        @pl.when(n >= 2)
        def _():
            wait_slot(1 - slot)
        wait_slot(slot)
        pad_rows(False)

        stage_ref[0] = jnp.zeros((tm, stage_ref.shape[2]), stage_ref.dtype)
        seg_end = tab_ref[3, E - 1]
        n_tail = (xs_ref.shape[0] - seg_end) // tm

        def tail_tiles(issue):
            def body(j, c):
                row0 = pl.multiple_of(seg_end + j * tm, tm)
                cp = pltpu.make_async_copy(stage_ref.at[0], xs_ref.at[pl.ds(row0, tm), :], zsem)
                if issue:
                    cp.start()
                else:
                    cp.wait()
                return c
            lax.fori_loop(0, n_tail, body, 0)

        tail_tiles(True)
        tail_tiles(False)


def _dispatch(pos, tile_tab, h, T, n_tiles, E):
    N, D = h.shape
    tm = _pick(T, 512)
    assert N % tm == 0
    P = n_tiles * T
    return pl.pallas_call(
        functools.partial(_dispatch_kernel, E=E),
        grid=(N // tm,),
        in_specs=[
            pl.BlockSpec((8, tm), lambda i: (0, i), memory_space=pltpu.SMEM),
            pl.BlockSpec(memory_space=pltpu.SMEM),
            pl.BlockSpec((tm, D), lambda i: (i, 0)),
        ],
        out_specs=pl.BlockSpec(memory_space=pl.ANY),
        out_shape=jax.ShapeDtypeStruct((P, D), h.dtype),
        scratch_shapes=[pltpu.VMEM((2, tm, D), h.dtype), pltpu.VMEM((8, D), h.dtype),
                        pltpu.SemaphoreType.DMA((2,)), pltpu.SemaphoreType.DMA(())],
        compiler_params=_params(("arbitrary",), 32),
        name="moe_dispatch",
    )(pos, tile_tab, h)


def _combine_kernel(pos_ref, posn_ref, w_ref, y_ref, x_ref, gate_ref, g_ref, sc_ref, sh_ref, xo_ref, h_ref,
                    buf_ref, sem):
    i = pl.program_id(0)
    n = pl.num_programs(0)
    tm = x_ref.shape[0]
    slot = lax.rem(i, 2)

    def row_copy(s, r, k, src_row):
        return pltpu.make_async_copy(y_ref.at[pl.ds(src_row, 1), :], buf_ref.at[s, k, pl.ds(r, 1), :], sem.at[s])

    def gather(p_ref, s):
        def body(r, c):
            for k in range(TOP_K):
                row_copy(s, r, k, p_ref[k, r]).start(priority=k)
            return c
        lax.fori_loop(0, tm, body, 0, unroll=_DMA_UNROLL)

    @pl.when(i == 0)
    def _():
        gather(pos_ref, 0)

    @pl.when(i + 1 < n)
    def _():
        gather(posn_ref, 1 - slot)

    def wait(r, c):
        for k in range(TOP_K):
            row_copy(slot, r, k, 0).wait()
        return c

    lax.fori_loop(0, tm, wait, 0, unroll=_DMA_UNROLL)
    w = w_ref[...]
    moe = w[:, 0:1] * _unpack_halves(buf_ref[slot, 0]) + w[:, 1:2] * _unpack_halves(buf_ref[slot, 1])
    xn = x_ref[...] + gate_ref[...] * moe
    xo_ref[...] = xn
    h_ref[...] = _norm_mod(xn, g_ref[...], sc_ref[...], sh_ref[...]).astype(h_ref.dtype)


def _combine(pos, w_col, y, x2, gate, g, scale, shift, S, h_dtype):
    N, D = x2.shape
    tm = _pick(S, 512)
    row = pl.BlockSpec((tm, D), lambda i: (i, 0))
    last = N // tm - 1
    return pl.pallas_call(
        _combine_kernel,
        grid=(N // tm,),
        in_specs=[
            pl.BlockSpec((8, tm), lambda i: (0, i), memory_space=pltpu.SMEM),
            pl.BlockSpec((8, tm), lambda i: (0, jnp.minimum(i + 1, last)), memory_space=pltpu.SMEM),
            pl.BlockSpec((tm, TOP_K), lambda i: (i, 0)),
            pl.BlockSpec(memory_space=pl.ANY),
            row,
            _row_vec_spec(D, tm, S),
            pl.BlockSpec((1, D), lambda i: (0, 0)),
            _row_vec_spec(D, tm, S),
            _row_vec_spec(D, tm, S),
        ],
        out_specs=[row, row],
        out_shape=[jax.ShapeDtypeStruct((N, D), F32), jax.ShapeDtypeStruct((N, D), h_dtype)],
        scratch_shapes=[pltpu.VMEM((2, TOP_K, tm, D // 2), jnp.uint32), pltpu.SemaphoreType.DMA((2,))],
        compiler_params=_params(("arbitrary",), 48),
        name="moe_combine",
    )(pos, pos, w_col, y, x2, gate, g, scale, shift)


def _split_hi_lo(w):
    hi = w.astype(BF16)
    lo = (w - hi.astype(F32)).astype(BF16)
    return hi, lo


def kernel(x, c, rel_bias, ada_w, ada_b, norm_g, a_wqkv, a_q_norm, a_k_norm, a_sinks, a_wo, b_wdown, b_q_a_norm, b_kv_a_norm, b_wuq, b_wukv, b_q_norm, b_k_norm, b_wo, ffn_wg, ffn_wu, ffn_wd, moe_router, moe_router_b, moe_wg, moe_wu, moe_wd):
    B, S, D = x.shape
    N = B * S
    depth = ada_w.shape[0]
    n_mod = ada_w.shape[2] // D
    Dh = a_q_norm.shape[-1]
    H = a_sinks.shape[-1]
    KV = (a_wqkv.shape[-1] // Dh - H) // 2
    G = H // KV
    QL = b_q_a_norm.shape[-1]
    KVL = b_kv_a_norm.shape[-1]
    R = b_wdown.shape[-1] - QL - KVL
    NOPE = b_q_norm.shape[-1] - R
    HB = b_wuq.shape[-1] // (NOPE + R)
    V = b_wukv.shape[-1] // HB - NOPE
    E = moe_router.shape[-1]

    rows = ((B + 15) // 16) * 16
    c_pad = jnp.zeros((rows, D), F32).at[:B].set(c)
    mod = _adaln(c_pad, ada_w, ada_b)[:, :B].reshape(depth, B, n_mod, D)
    mod_vec = lambda layer, j: mod[layer, :, j].reshape(B, 1, D)
    gain = lambda layer, j: norm_g[layer, j].reshape(1, D)

    bias_tab = _swa_bias_tables(rel_bias, KV, G)
    pos_f = jnp.arange(S, dtype=F32)
    inv = ROPE_THETA ** (-jnp.arange(0, R, 2, dtype=F32) / R)
    ang = pos_f[:, None] * inv[None, :]
    cos, sin = jnp.cos(ang), jnp.sin(ang)
    cos2 = jnp.concatenate([cos, cos], axis=-1)
    sin2 = jnp.concatenate([-sin, sin], axis=-1)
    swap = lambda t: jnp.concatenate([t[..., R // 2:], t[..., :R // 2]], axis=-1)

    x2 = x.reshape(N, D)
    h = None
    one_tile = jnp.zeros((N,), jnp.int32)
    ffn_w = [w.astype(BF16)[:, None] for w in (ffn_wg, ffn_wu, ffn_wd)]
    moe_w = (moe_wg, moe_wu, moe_wd)
    F_moe = moe_wg.shape[-1]

    for layer in range(depth):
        i = layer // 2
        moe_layer = layer % 2 == 1
        h2_dtype = F32 if moe_layer else BF16
        if not moe_layer:
            if layer == 0:
                qkv = _matmul(x2, a_wqkv[i].astype(BF16), S, norm=(gain(0, 0), mod_vec(0, 1), mod_vec(0, 0)))
            else:
                qkv = _matmul(h, a_wqkv[i].astype(BF16), S)
            o = _swa(qkv, bias_tab, a_q_norm[i], a_k_norm[i], a_sinks[i], B, S, H, KV, Dh)
            wo = a_wo[i]
        else:
            wd_ext = jnp.concatenate([b_wdown[i], swap(b_wdown[i][:, QL + KVL:])], axis=1).astype(BF16)
            kg = b_k_norm[i]
            cq, ckv, kpe = _mla_down(h, wd_ext, b_q_a_norm[i], b_kv_a_norm[i], kg[NOPE:], swap(kg[NOPE:]),
                                     cos2, sin2, S, QL, KVL, R)
            wq = b_wuq[i].reshape(QL, HB, NOPE + R)
            wq_h = jnp.concatenate([wq, swap(wq[..., NOPE:])], axis=-1).transpose(1, 0, 2).astype(BF16)
            wkv_h = b_wukv[i].reshape(KVL, HB, NOPE + V).transpose(1, 0, 2).astype(BF16)
            qg = b_q_norm[i]
            qf, kf, vf = _mla_up(cq, ckv, kpe, cos2, sin2, wq_h, wkv_h, qg[:NOPE], qg[NOPE:], swap(qg[NOPE:]),
                                 kg[:NOPE], B, S, NOPE, R, V, _pick(S, 512))
            o = _flash(qf, kf, vf)
            wo = b_wo[i]
        x2, h2 = _proj_res(o, wo.astype(BF16), x2, mod_vec(layer, 2), gain(layer, 1), mod_vec(layer, 4),
                           mod_vec(layer, 3), S, h2_dtype)

        nxt = min(layer + 1, depth - 1)
        nxt_args = (gain(nxt, 0), mod_vec(nxt, 1), mod_vec(nxt, 0), S, BF16)
        if not moe_layer:
            tm = _pick(S, 512)
            all_tiles = jnp.concatenate([jnp.full((1,), N // tm, jnp.int32), jnp.full((N // tm,), tm, jnp.int32)])
            x2, h = _ffn(h2, *ffn_w, i, one_tile[:N // tm], all_tiles, tm,
                         _pick(ffn_wg.shape[-1], 512), epilogue=(x2, mod_vec(layer, 5)) + nxt_args)
        else:
            T = _pick(N // E, 1024)
            n_tiles = (N * TOP_K) // T + E
            w_hi, w_lo = _split_hi_lo(moe_router[i].T)
            meta_i, meta_w, counts = _router(h2, jnp.concatenate([w_hi, w_lo], axis=0), moe_router_b[i])
            pos, tile_tab = _positions(meta_i, counts, T, n_tiles)
            xs = _dispatch(pos, tile_tab, h2, T, n_tiles, E)
            tile_use = jnp.concatenate([tile_tab[1, :1], tile_tab[4, :n_tiles]])
            y = _ffn(xs, *moe_w, i, tile_tab[0, :n_tiles], tile_use, T, _pick(F_moe, 256),
                     nsplit=2 if T >= 256 else 1)
            x2, h = _combine(pos, meta_w[:TOP_K].T, y, x2, mod_vec(layer, 5), *nxt_args)
    return x2.reshape(B, S, D)
```
